```python
import math
import jax, jax.numpy as jnp
from jax import lax
import numpy as np

D_MODEL = 1024
BATCH = 4
SEQ = 8192
DEPTH = 2

N_META = 16
D_RNN = D_MODEL
LRU_HEADS = 16
LRU_HEAD_DIM = D_RNN // LRU_HEADS
CONV_A = 4
LRU_C = 8.0
D_CONV = D_MODEL
CONV_B = 3
N_GROUPS = 4
EXPERTS_PER_GROUP = 8
N_EXPERTS = N_GROUPS * EXPERTS_PER_GROUP
TOP_K = 2
D_EXPERT = D_MODEL // 2
BLK = 128
N_MIXERS = 2
N_A_LAYERS = (DEPTH + 1) // 2
N_B_LAYERS = DEPTH // 2
ALPHA = (2.0 * DEPTH) ** 0.25
BETA = (8.0 * DEPTH) ** -0.25
LN_EPS = 1e-5

kernel_name = "hybrid_rglru_shortconv_hmoe_deepnorm"


def layer_norm(x, g, b):
    xf = x.astype(jnp.float32)
    mu = jnp.mean(xf, axis=-1, keepdims=True)
    var = jnp.mean(jnp.square(xf - mu), axis=-1, keepdims=True)
    y = (xf - mu) * lax.rsqrt(var + LN_EPS) * g.astype(jnp.float32) + b.astype(jnp.float32)
    return y.astype(x.dtype)


def causal_depthwise_conv(x, w):
    K, C = w.shape
    return lax.conv_general_dilated(
        x, w[:, None, :].astype(x.dtype), window_strides=(1,), padding=[(K - 1, 0)],
        dimension_numbers=("NWC", "WIO", "NWC"), feature_group_count=C)


def _linear_recurrence_combine(c1, c2):
    a1, b1 = c1
    a2, b2 = c2
    return a1 * a2, a2 * b1 + b2


def rglru_mixer(h, w_in, conv_w, conv_b, w_a, b_a, w_i, b_i, lam, w_out):
    Bsz, S, _ = h.shape
    u = h @ w_in
    y_br = jax.nn.gelu(u[..., :D_RNN])
    xc = causal_depthwise_conv(u[..., D_RNN:], conv_w) + conv_b
    xh = xc.reshape(Bsz, S, LRU_HEADS, LRU_HEAD_DIM)
    r = jax.nn.sigmoid(jnp.einsum("bshd,hde->bshe", xh, w_a).reshape(Bsz, S, D_RNN) + b_a)
    gi = jax.nn.sigmoid(jnp.einsum("bshd,hde->bshe", xh, w_i).reshape(Bsz, S, D_RNN) + b_i)
    log_a = -LRU_C * r.astype(jnp.float32) * jax.nn.softplus(-lam.astype(jnp.float32))
    a = jnp.exp(log_a)
    mult = jnp.sqrt(-jnp.expm1(2.0 * log_a))
    b = mult * (gi * xc).astype(jnp.float32)
    _, hs = lax.associative_scan(_linear_recurrence_combine, (a, b), axis=1)
    return (hs.astype(h.dtype) * y_br) @ w_out


def shortconv_mixer(h, w_in, conv_w, w_out):
    u = h @ w_in
    bg = u[..., :D_CONV]
    cg = u[..., D_CONV:2 * D_CONV]
    v = u[..., 2 * D_CONV:]
    zc = causal_depthwise_conv(cg * v, conv_w)
    return (bg * zc) @ w_out


def hierarchical_moe(h, w_group, b_group, w_expert, b_expert, w_gate, w_up, w_down):
    Bsz, S, D = h.shape
    x2 = h.reshape(Bsz * S, D)
    T = x2.shape[0]
    xf = x2.astype(jnp.float32)
    glog = xf @ w_group.astype(jnp.float32) + b_group.astype(jnp.float32)
    gprob = jax.nn.softmax(glog, axis=-1)
    g = jnp.argmax(glog, axis=-1)
    pg = jnp.take_along_axis(gprob, g[:, None], axis=1)
    elog = (xf @ w_expert.astype(jnp.float32)).reshape(T, N_GROUPS, EXPERTS_PER_GROUP) + b_expert.astype(jnp.float32)
    elog_g = jnp.take_along_axis(elog, g[:, None, None], axis=1)[:, 0]
    vals, loc = lax.top_k(elog_g, TOP_K)
    gates = pg * jax.nn.softmax(vals, axis=-1)
    eid = g[:, None] * EXPERTS_PER_GROUP + loc
    A = T * TOP_K
    eid_f = eid.reshape(-1)
    tok_f = jnp.repeat(jnp.arange(T, dtype=jnp.int32), TOP_K)
    gate_f = gates.reshape(-1)
    order = jnp.argsort(eid_f)
    eid_s = eid_f[order]
    tok_s = tok_f[order]
    gate_s = gate_f[order]
    counts = jnp.bincount(eid_f, length=N_EXPERTS)
    start = jnp.cumsum(counts) - counts
    padded = (counts + BLK - 1) // BLK * BLK
    pend = jnp.cumsum(padded)
    pstart = pend - padded
    rank = jnp.arange(A, dtype=jnp.int32) - start[eid_s]
    dest = pstart[eid_s] + rank
    n_blocks = (A + N_EXPERTS * (BLK - 1) + BLK - 1) // BLK
    P = n_blocks * BLK
    slot_tok = jnp.full((P,), T, dtype=jnp.int32).at[dest].set(tok_s)
    slot_gate = jnp.zeros((P,), jnp.float32).at[dest].set(gate_s)
    block_expert = jnp.clip(jnp.searchsorted(pend, jnp.arange(n_blocks) * BLK, side="right"), 0, N_EXPERTS - 1)
    xpad = jnp.concatenate([x2, jnp.zeros((1, D), x2.dtype)], axis=0)
    xs = xpad[slot_tok].reshape(n_blocks, BLK, D)

    def expert_block(args):
        xb, e = args
        hb = jax.nn.silu(xb @ w_gate[e]) * (xb @ w_up[e])
        return hb @ w_down[e]

    ys = lax.map(expert_block, (xs, block_expert)).reshape(P, D)
    y = jnp.zeros((T + 1, D), x2.dtype).at[slot_tok].add(ys * slot_gate[:, None].astype(ys.dtype))
    return y[:T].reshape(Bsz, S, D)


def setup_inputs(seed: int = 0) -> dict:
    key = jax.random.key(seed)
    ks = jax.random.split(key, 24)

    def nrm(k, shape, scale):
        return jax.random.normal(k, shape, jnp.float32) * scale

    u = jax.random.uniform(ks[10], (N_A_LAYERS, D_RNN), jnp.float32, 0.9, 0.999)
    a0 = u ** (1.0 / LRU_C)
    lru_lambda = jnp.log(a0) - jnp.log1p(-a0)
    return {
        "x": nrm(ks[0], (BATCH, SEQ, D_MODEL), 1.0),
        "meta_tokens": nrm(ks[1], (N_META, D_MODEL), 1.0),
        "lru_w_in": nrm(ks[2], (N_A_LAYERS, D_MODEL, 2 * D_RNN), D_MODEL ** -0.5),
        "lru_conv_w": nrm(ks[3], (N_A_LAYERS, CONV_A, D_RNN), CONV_A ** -0.5),
        "lru_conv_b": nrm(ks[4], (N_A_LAYERS, D_RNN), 0.01),
        "lru_w_a": nrm(ks[5], (N_A_LAYERS, LRU_HEADS, LRU_HEAD_DIM, LRU_HEAD_DIM), LRU_HEAD_DIM ** -0.5),
        "lru_b_a": nrm(ks[6], (N_A_LAYERS, D_RNN), 0.01),
        "lru_w_i": nrm(ks[7], (N_A_LAYERS, LRU_HEADS, LRU_HEAD_DIM, LRU_HEAD_DIM), LRU_HEAD_DIM ** -0.5),
        "lru_b_i": nrm(ks[8], (N_A_LAYERS, D_RNN), 0.01),
        "lru_lambda": lru_lambda,
        "lru_w_out": nrm(ks[9], (N_A_LAYERS, D_RNN, D_MODEL), BETA * D_RNN ** -0.5),
        "sc_w_in": nrm(ks[11], (N_B_LAYERS, D_MODEL, 3 * D_CONV), D_MODEL ** -0.5),
        "sc_conv_w": nrm(ks[12], (N_B_LAYERS, CONV_B, D_CONV), CONV_B ** -0.5),
        "sc_w_out": nrm(ks[13], (N_B_LAYERS, D_CONV, D_MODEL), BETA * D_CONV ** -0.5),
        "moe_w_group": nrm(ks[14], (DEPTH, D_MODEL, N_GROUPS), D_MODEL ** -0.5),
        "moe_b_group": nrm(ks[15], (DEPTH, N_GROUPS), 0.01),
        "moe_w_expert": nrm(ks[16], (DEPTH, D_MODEL, N_EXPERTS), D_MODEL ** -0.5),
        "moe_b_expert": nrm(ks[17], (DEPTH, N_GROUPS, EXPERTS_PER_GROUP), 0.01),
        "moe_w_gate": nrm(ks[18], (DEPTH, N_EXPERTS, D_MODEL, D_EXPERT), D_MODEL ** -0.5),
        "moe_w_up": nrm(ks[19], (DEPTH, N_EXPERTS, D_MODEL, D_EXPERT), D_MODEL ** -0.5),
        "moe_w_down": nrm(ks[20], (DEPTH, N_EXPERTS, D_EXPERT, D_MODEL), BETA * D_EXPERT ** -0.5),
        "ln_g": 1.0 + nrm(ks[21], (DEPTH, 2, D_MODEL), 0.02),
        "ln_b": nrm(ks[22], (DEPTH, 2, D_MODEL), 0.02),
    }


def reference(x, meta_tokens, lru_w_in, lru_conv_w, lru_conv_b, lru_w_a, lru_b_a, lru_w_i, lru_b_i,
              lru_lambda, lru_w_out, sc_w_in, sc_conv_w, sc_w_out, moe_w_group, moe_b_group,
              moe_w_expert, moe_b_expert, moe_w_gate, moe_w_up, moe_w_down, ln_g, ln_b):
    Bsz = x.shape[0]
    meta = jnp.broadcast_to(meta_tokens.astype(x.dtype)[None], (Bsz, N_META, D_MODEL))
    h = jnp.concatenate([meta, x], axis=1)
    for i in range(DEPTH):
        j = i // N_MIXERS
        if i % N_MIXERS == 0:
            mixed = rglru_mixer(h, lru_w_in[j], lru_conv_w[j], lru_conv_b[j], lru_w_a[j], lru_b_a[j],
                                lru_w_i[j], lru_b_i[j], lru_lambda[j], lru_w_out[j])
        else:
            mixed = shortconv_mixer(h, sc_w_in[j], sc_conv_w[j], sc_w_out[j])
        h = layer_norm(ALPHA * h + mixed, ln_g[i, 0], ln_b[i, 0])
        ffn = hierarchical_moe(h, moe_w_group[i], moe_b_group[i], moe_w_expert[i], moe_b_expert[i],
                               moe_w_gate[i], moe_w_up[i], moe_w_down[i])
        h = layer_norm(ALPHA * h + ffn, ln_g[i, 1], ln_b[i, 1])
    return h[:, N_META:]
```

```python
import functools

import jax
import jax.numpy as jnp
from jax import lax
from jax.experimental import pallas as pl
from jax.experimental.pallas import tpu as pltpu

F32 = jnp.float32
BF16 = jnp.bfloat16

D = 1024
N_META_TOK = 16
SEQ_TOT = 8192 + N_META_TOK
LRU_HEADS = 16
HEAD_DIM = D // LRU_HEADS
LRU_C = 8.0
N_GROUPS = 4
EPG = 8
N_EXPERTS = N_GROUPS * EPG
D_EXPERT = D // 2
DEPTH = 2
ALPHA = (2.0 * DEPTH) ** 0.25
LN_EPS = 1e-5

LANES = 128
SUBLANES = 8
GATE_BLK = 256

TS = 432
TT = 432
BM = 512
CH = 16

VMEM_LIMIT = 56 * 1024 * 1024


def _softplus(x):
    return jnp.maximum(x, 0.0) + jnp.log1p(jnp.exp(-jnp.abs(x)))


def _shifted(cur, prv, s, sub):
    return jnp.where(sub >= s, pltpu.roll(cur, s, 0), pltpu.roll(prv, s, 0))


def _ln_route_epilogue(h_ref, m_scr, lng_ref, lnb_ref, wr_ref, br_ref,
                       h1_ref, route_ref, cnt_ref, h1b_scr, tri_scr, cnt_scr, first):
    ts = m_scr.shape[0]

    @pl.when(first)
    def _():
        row = lax.broadcasted_iota(jnp.int32, (ts, ts), 0)
        col = lax.broadcasted_iota(jnp.int32, (ts, ts), 1)
        tri_scr[...] = jnp.where(col < row, 1.0, 0.0).astype(BF16)
        cnt_scr[...] = jnp.zeros_like(cnt_scr)

    g = lng_ref[...]
    bta = lnb_ref[...]

    def ln_body(i, c):
        r = pl.multiple_of(i * CH, CH)
        z = ALPHA * h_ref[0, pl.ds(r, CH), :] + m_scr[pl.ds(r, CH), :]
        mu = jnp.mean(z, axis=-1, keepdims=True)
        zc = z - mu
        var = jnp.mean(zc * zc, axis=-1, keepdims=True)
        y = zc * lax.rsqrt(var + LN_EPS) * g + bta
        h1_ref[0, pl.ds(r, CH), :] = y
        h1b_scr[pl.ds(r, CH), :] = y.astype(BF16)
        return c

    lax.fori_loop(0, ts // CH, ln_body, 0)

    logits = jnp.dot(h1b_scr[...], wr_ref[...], preferred_element_type=F32) + br_ref[...]
    lane = lax.broadcasted_iota(jnp.int32, (ts, LANES), 1).astype(F32)
    neg = jnp.float32(-jnp.inf)
    big = jnp.float32(LANES)

    gmask = lane < N_GROUPS
    gl = jnp.where(gmask, logits, neg)
    gmax = jnp.max(gl, axis=-1, keepdims=True)
    gidx = jnp.min(jnp.where(gl == gmax, lane, big), axis=-1, keepdims=True)
    sumexp = jnp.sum(jnp.where(gmask, jnp.exp(gl - gmax), 0.0), axis=-1, keepdims=True)
    pg = 1.0 / sumexp

    lo = N_GROUPS + EPG * gidx
    emask = (lane >= lo) & (lane < lo + EPG)
    el = jnp.where(emask, logits, neg)
    v1 = jnp.max(el, axis=-1, keepdims=True)
    i1 = jnp.min(jnp.where(el == v1, lane, big), axis=-1, keepdims=True)
    el2 = jnp.where(lane == i1, neg, el)
    v2 = jnp.max(el2, axis=-1, keepdims=True)
    i2 = jnp.min(jnp.where(el2 == v2, lane, big), axis=-1, keepdims=True)
    ex = jnp.exp(v2 - v1)
    den = 1.0 / (1.0 + ex)
    g1 = pg * den
    g2 = pg * ex * den
    e1 = i1 - N_GROUPS
    e2 = i2 - N_GROUPS

    sel1 = lane == e1
    sel2 = lane == e2
    onehot = jnp.where(sel1 | sel2, 1.0, 0.0)
    before = jnp.dot(tri_scr[...], onehot.astype(BF16), preferred_element_type=F32) + cnt_scr[...]
    rank1 = jnp.sum(jnp.where(sel1, before, 0.0), axis=-1, keepdims=True)
    rank2 = jnp.sum(jnp.where(sel2, before, 0.0), axis=-1, keepdims=True)
    cnt_scr[...] = cnt_scr[...] + jnp.sum(onehot, axis=0, keepdims=True)

    slab = jnp.where(lane == 0, e1, 0.0)
    slab = jnp.where(lane == 1, e2, slab)
    slab = jnp.where(lane == 2, rank1, slab)
    slab = jnp.where(lane == 3, rank2, slab)
    slab = jnp.where(lane == 4, g1, slab)
    slab = jnp.where(lane == 5, g2, slab)
    route_ref[0] = slab
    cnt_ref[...] = cnt_scr[...]


def _rglru_kernel(h_ref, win_ref, cw_ref, cb_ref, wa_ref, wi_ref, ba_ref, bi_ref, lam_ref, wout_ref,
                  lng_ref, lnb_ref, wr_ref, br_ref,
                  h1_ref, route_ref, cnt_ref,
                  u_scr, xc_scr, xcb_scr, g_scr, y_scr, halo_scr, carry_scr, tri_scr, cnt_scr):
    b = pl.program_id(0)
    j = pl.program_id(1)
    ts = xc_scr.shape[0]

    @pl.when(j == 0)
    def _():
        halo_scr[...] = jnp.zeros_like(halo_scr)
        carry_scr[...] = jnp.zeros_like(carry_scr)

    u_scr[...] = jnp.dot(h_ref[0].astype(BF16), win_ref[...], preferred_element_type=F32)

    sub = lax.broadcasted_iota(jnp.int32, (SUBLANES, D), 0)
    cw = [cw_ref[k:k + 1, :] for k in range(4)]
    cb = cb_ref[...]

    def conv_body(i, prv):
        r = pl.multiple_of(i * CH, CH)
        outs = []
        for hh in range(CH // SUBLANES):
            cur = u_scr[pl.ds(r + SUBLANES * hh, SUBLANES), D:]
            acc = cur * cw[3] + cb
            for s in (1, 2, 3):
                acc = acc + _shifted(cur, prv, s, sub) * cw[3 - s]
            outs.append(acc)
            prv = cur
        xc = jnp.concatenate(outs, axis=0)
        xc_scr[pl.ds(r, CH), :] = xc
        xcb_scr[pl.ds(r, CH), :] = xc.astype(BF16)
        return prv

    halo_scr[...] = lax.fori_loop(0, ts // CH, conv_body, halo_scr[...])

    for k in range(D // GATE_BLK):
        xk = xcb_scr[:, k * GATE_BLK:(k + 1) * GATE_BLK]
        g_scr[:, k * GATE_BLK:(k + 1) * GATE_BLK] = jnp.dot(xk, wa_ref[k], preferred_element_type=F32)
        g_scr[:, D + k * GATE_BLK:D + (k + 1) * GATE_BLK] = jnp.dot(xk, wi_ref[k], preferred_element_type=F32)

    ba = ba_ref[...]
    bi = bi_ref[...]
    nsp = -LRU_C * _softplus(-lam_ref[...])

    def scan_body(i, c):
        r = pl.multiple_of(i * CH, CH)
        ys = []
        for hh in range(CH // SUBLANES):
            r0 = r + SUBLANES * hh
            xc = xc_scr[pl.ds(r0, SUBLANES), :]
            ra = jax.nn.sigmoid(g_scr[pl.ds(r0, SUBLANES), :D] + ba)
            gi = jax.nn.sigmoid(g_scr[pl.ds(r0, SUBLANES), D:] + bi)
            log_a = ra * nsp
            a = jnp.exp(log_a)
            m2 = -jnp.tanh(log_a) * (a * a + 1.0)
            bv = jnp.sqrt(m2) * (gi * xc)
            av = a
            for s in (1, 2, 4):
                m = sub >= s
                a_s = pltpu.roll(av, s, 0)
                b_s = pltpu.roll(bv, s, 0)
                bv = jnp.where(m, av * b_s + bv, bv)
                av = jnp.where(m, av * a_s, av)
            hs = av * c + bv
            c = jnp.broadcast_to(hs[SUBLANES - 1:SUBLANES, :], (SUBLANES, D))
            ys.append(hs * jax.nn.gelu(u_scr[pl.ds(r0, SUBLANES), :D]))
        y_scr[pl.ds(r, CH), :] = jnp.concatenate(ys, axis=0).astype(BF16)
        return c

    carry_scr[...] = lax.fori_loop(0, ts // CH, scan_body, carry_scr[...])

    xc_scr[...] = jnp.dot(y_scr[...], wout_ref[...], preferred_element_type=F32)
    _ln_route_epilogue(h_ref, xc_scr, lng_ref, lnb_ref, wr_ref, br_ref,
                       h1_ref, route_ref, cnt_ref, xcb_scr, tri_scr, cnt_scr,
                       (b == 0) & (j == 0))


def _shortconv_kernel(h_ref, win_ref, cw_ref, wout_ref, lng_ref, lnb_ref, wr_ref, br_ref,
                      h1_ref, route_ref, cnt_ref,
                      u_scr, m_scr, y_scr, halo_scr, tri_scr, cnt_scr):
    b = pl.program_id(0)
    j = pl.program_id(1)
    ts = m_scr.shape[0]

    @pl.when(j == 0)
    def _():
        halo_scr[...] = jnp.zeros_like(halo_scr)

    u_scr[...] = jnp.dot(h_ref[0].astype(BF16), win_ref[...], preferred_element_type=F32)

    sub = lax.broadcasted_iota(jnp.int32, (SUBLANES, D), 0)
    cw = [cw_ref[k:k + 1, :] for k in range(3)]

    def conv_body(i, prv):
        r = pl.multiple_of(i * CH, CH)
        ys = []
        for hh in range(CH // SUBLANES):
            r0 = r + SUBLANES * hh
            cur = u_scr[pl.ds(r0, SUBLANES), D:2 * D] * u_scr[pl.ds(r0, SUBLANES), 2 * D:]
            acc = cur * cw[2]
            for s in (1, 2):
                acc = acc + _shifted(cur, prv, s, sub) * cw[2 - s]
            ys.append(u_scr[pl.ds(r0, SUBLANES), :D] * acc)
            prv = cur
        y_scr[pl.ds(r, CH), :] = jnp.concatenate(ys, axis=0).astype(BF16)
        return prv

    halo_scr[...] = lax.fori_loop(0, ts // CH, conv_body, halo_scr[...])

    m_scr[...] = jnp.dot(y_scr[...], wout_ref[...], preferred_element_type=F32)
    _ln_route_epilogue(h_ref, m_scr, lng_ref, lnb_ref, wr_ref, br_ref,
                       h1_ref, route_ref, cnt_ref, y_scr, tri_scr, cnt_scr,
                       (b == 0) & (j == 0))


def _const_spec(shape):
    n = len(shape)
    return pl.BlockSpec(shape, lambda b, j: (0,) * n)


def _mixer_out(batch):
    nj = SEQ_TOT // TS
    out_shape = (jax.ShapeDtypeStruct((batch, SEQ_TOT, D), F32),
                 jax.ShapeDtypeStruct((batch, SEQ_TOT, LANES), F32),
                 jax.ShapeDtypeStruct((1, LANES), F32))
    out_specs = (pl.BlockSpec((1, TS, D), lambda b, j: (b, j, 0)),
                 pl.BlockSpec((1, TS, LANES), lambda b, j: (b, j, 0)),
                 pl.BlockSpec((1, LANES), lambda b, j: (0, 0)))
    return nj, out_shape, out_specs


def _rglru_layer(h, win, cw, cb, wa, wi, ba, bi, lam, wout, lng, lnb, wr, br):
    batch = h.shape[0]
    nj, out_shape, out_specs = _mixer_out(batch)
    consts = (win, cw, cb, wa, wi, ba, bi, lam, wout, lng, lnb, wr, br)
    return pl.pallas_call(
        _rglru_kernel,
        grid=(batch, nj),
        in_specs=[pl.BlockSpec((1, TS, D), lambda b, j: (b, j, 0))] + [_const_spec(c.shape) for c in consts],
        out_specs=out_specs,
        out_shape=out_shape,
        scratch_shapes=[
            pltpu.VMEM((TS, 2 * D), F32),
            pltpu.VMEM((TS, D), F32),
            pltpu.VMEM((TS, D), BF16),
            pltpu.VMEM((TS, 2 * D), F32),
            pltpu.VMEM((TS, D), BF16),
            pltpu.VMEM((SUBLANES, D), F32),
            pltpu.VMEM((SUBLANES, D), F32),
            pltpu.VMEM((TS, TS), BF16),
            pltpu.VMEM((1, LANES), F32),
        ],
        compiler_params=pltpu.CompilerParams(
            dimension_semantics=("arbitrary", "arbitrary"), vmem_limit_bytes=VMEM_LIMIT),
        name="rglru_layer",
    )(h, *consts)


def _shortconv_layer(h, win, cw, wout, lng, lnb, wr, br):
    batch = h.shape[0]
    nj, out_shape, out_specs = _mixer_out(batch)
    consts = (win, cw, wout, lng, lnb, wr, br)
    return pl.pallas_call(
        _shortconv_kernel,
        grid=(batch, nj),
        in_specs=[pl.BlockSpec((1, TS, D), lambda b, j: (b, j, 0))] + [_const_spec(c.shape) for c in consts],
        out_specs=out_specs,
        out_shape=out_shape,
        scratch_shapes=[
            pltpu.VMEM((TS, 3 * D), F32),
            pltpu.VMEM((TS, D), F32),
            pltpu.VMEM((TS, D), BF16),
            pltpu.VMEM((SUBLANES, D), F32),
            pltpu.VMEM((TS, TS), BF16),
            pltpu.VMEM((1, LANES), F32),
        ],
        compiler_params=pltpu.CompilerParams(
            dimension_semantics=("arbitrary", "arbitrary"), vmem_limit_bytes=VMEM_LIMIT),
        name="shortconv_layer",
    )(h, *consts)


def _dispatch_kernel(pend_ref, padded_ref, dest_ref, h_ref, xs_ref, zero_scr, sem_z, sem):
    i = pl.program_id(0)
    tt = h_ref.shape[0]

    @pl.when(i == 0)
    def _():
        zero_scr[...] = jnp.zeros_like(zero_scr)
        for e in range(N_EXPERTS):
            @pl.when(padded_ref[e] > 0)
            def _():
                start = pl.multiple_of(pend_ref[e] - BM, BM)
                pltpu.make_async_copy(zero_scr, xs_ref.at[pl.ds(start, BM), :], sem_z).start()
        for e in range(N_EXPERTS):
            @pl.when(padded_ref[e] > 0)
            def _():
                pltpu.make_async_copy(zero_scr, xs_ref.at[pl.ds(0, BM), :], sem_z).wait()

    def row_copy(t, d):
        return pltpu.make_async_copy(h_ref.at[pl.ds(t, 1), :], xs_ref.at[pl.ds(d, 1), :], sem)

    def issue(t, c):
        row_copy(t, dest_ref[0, 0, 2 * t]).start()
        row_copy(t, dest_ref[0, 0, 2 * t + 1]).start()
        return c

    lax.fori_loop(0, tt, issue, 0)

    def drain(t, c):
        row_copy(0, 0).wait()
        return c

    lax.fori_loop(0, 2 * tt, drain, 0)


def _dispatch(h1, dest, pend, padded, n_rows):
    t_tot = h1.shape[0]
    nt = t_tot // TT
    return pl.pallas_call(
        _dispatch_kernel,
        grid_spec=pltpu.PrefetchScalarGridSpec(
            num_scalar_prefetch=2,
            grid=(nt,),
            in_specs=[pl.BlockSpec((1, 1, 2 * TT), lambda i, pe, pa: (i, 0, 0), memory_space=pltpu.SMEM),
                      pl.BlockSpec((TT, D), lambda i, pe, pa: (i, 0))],
            out_specs=pl.BlockSpec(memory_space=pl.ANY),
            scratch_shapes=[pltpu.VMEM((BM, D), F32),
                            pltpu.SemaphoreType.DMA(()),
                            pltpu.SemaphoreType.DMA(())],
        ),
        out_shape=jax.ShapeDtypeStruct((n_rows, D), F32),
        compiler_params=pltpu.CompilerParams(dimension_semantics=("arbitrary",)),
        name="moe_dispatch",
    )(pend, padded, dest.reshape(nt, 1, 2 * TT), h1)


def _expert_kernel(be_ref, nu_ref, xs_ref, wg_ref, wu_ref, wd_ref, ys_ref):
    i = pl.program_id(0)

    @pl.when(i < nu_ref[0])
    def _():
        xb = xs_ref[...].astype(BF16)
        g = jnp.dot(xb, wg_ref[0], preferred_element_type=F32)
        u = jnp.dot(xb, wu_ref[0], preferred_element_type=F32)
        hmid = (jax.nn.silu(g) * u).astype(BF16)
        ys_ref[...] = jnp.dot(hmid, wd_ref[0], preferred_element_type=F32)

    @pl.when(i >= nu_ref[0])
    def _():
        ys_ref[...] = jnp.zeros_like(ys_ref)


def _experts(xs, block_expert, n_used, wg, wu, wd):
    nb = xs.shape[0] // BM
    return pl.pallas_call(
        _expert_kernel,
        grid_spec=pltpu.PrefetchScalarGridSpec(
            num_scalar_prefetch=2,
            grid=(nb,),
            in_specs=[pl.BlockSpec((BM, D), lambda i, be, nu: (jnp.minimum(i, nu[0] - 1), 0)),
                      pl.BlockSpec((1, D, D_EXPERT), lambda i, be, nu: (be[i], 0, 0)),
                      pl.BlockSpec((1, D, D_EXPERT), lambda i, be, nu: (be[i], 0, 0)),
                      pl.BlockSpec((1, D_EXPERT, D), lambda i, be, nu: (be[i], 0, 0))],
            out_specs=pl.BlockSpec((BM, D), lambda i, be, nu: (i, 0)),
        ),
        out_shape=jax.ShapeDtypeStruct(xs.shape, F32),
        compiler_params=pltpu.CompilerParams(
            dimension_semantics=("arbitrary",), vmem_limit_bytes=VMEM_LIMIT),
        name="moe_experts",
    )(block_expert, n_used, xs, wg, wu, wd)


def _combine_kernel(dest_ref, h_ref, route_ref, lng_ref, lnb_ref, ys_ref, out_ref, y0_scr, y1_scr, sem):
    tt = h_ref.shape[0]

    def row_copy(d, buf, t):
        return pltpu.make_async_copy(ys_ref.at[pl.ds(d, 1), :], buf.at[pl.ds(t, 1), :], sem)

    def issue(t, c):
        row_copy(dest_ref[0, 0, 2 * t], y0_scr, t).start()
        row_copy(dest_ref[0, 0, 2 * t + 1], y1_scr, t).start()
        return c

    lax.fori_loop(0, tt, issue, 0)

    def drain(t, c):
        row_copy(0, y0_scr, 0).wait()
        return c

    lax.fori_loop(0, 2 * tt, drain, 0)

    g = lng_ref[...]
    bta = lnb_ref[...]

    def ln_body(i, c):
        r = pl.multiple_of(i * SUBLANES, SUBLANES)
        rows = pl.ds(r, SUBLANES)
        g1 = route_ref[rows, 4:5]
        g2 = route_ref[rows, 5:6]
        z = ALPHA * h_ref[rows, :] + (g1 * y0_scr[rows, :] + g2 * y1_scr[rows, :])
        mu = jnp.mean(z, axis=-1, keepdims=True)
        zc = z - mu
        var = jnp.mean(zc * zc, axis=-1, keepdims=True)
        out_ref[rows, :] = zc * lax.rsqrt(var + LN_EPS) * g + bta
        return c

    lax.fori_loop(0, tt // SUBLANES, ln_body, 0)


def _combine(h1, route, dest, ys, lng, lnb):
    t_tot = h1.shape[0]
    nt = t_tot // TT
    return pl.pallas_call(
        _combine_kernel,
        grid=(nt,),
        in_specs=[pl.BlockSpec((1, 1, 2 * TT), lambda i: (i, 0, 0), memory_space=pltpu.SMEM),
                  pl.BlockSpec((TT, D), lambda i: (i, 0)),
                  pl.BlockSpec((TT, LANES), lambda i: (i, 0)),
                  pl.BlockSpec((1, D), lambda i: (0, 0)),
                  pl.BlockSpec((1, D), lambda i: (0, 0)),
                  pl.BlockSpec(memory_space=pl.ANY)],
        out_specs=pl.BlockSpec((TT, D), lambda i: (i, 0)),
        out_shape=jax.ShapeDtypeStruct((t_tot, D), F32),
        scratch_shapes=[pltpu.VMEM((TT, D), F32), pltpu.VMEM((TT, D), F32), pltpu.SemaphoreType.DMA(())],
        compiler_params=pltpu.CompilerParams(dimension_semantics=("arbitrary",)),
        name="moe_combine",
    )(dest.reshape(nt, 1, 2 * TT), h1, route, lng, lnb, ys)


def _moe(h1, route, cnt, wg, wu, wd, lng, lnb):
    t_tot = h1.shape[0]
    n_blocks = (2 * t_tot + N_EXPERTS * (BM - 1) + BM - 1) // BM
    eid = route[:, 0:2].astype(jnp.int32)
    rank = route[:, 2:4].astype(jnp.int32)
    counts = cnt[0, :N_EXPERTS].astype(jnp.int32)
    padded = (counts + BM - 1) // BM * BM
    pend = jnp.cumsum(padded)
    pstart = pend - padded
    dest = (pstart[eid] + rank).reshape(-1)
    n_used = pend[-1:] // BM
    blk = jnp.minimum(jnp.arange(n_blocks, dtype=jnp.int32), n_used[0] - 1) * BM
    block_expert = jnp.clip(jnp.searchsorted(pend, blk, side="right"), 0, N_EXPERTS - 1).astype(jnp.int32)

    xs = _dispatch(h1, dest, pend, padded, n_blocks * BM)
    ys = _experts(xs, block_expert, n_used, wg, wu, wd)
    return _combine(h1, route, dest, ys, lng, lnb)


def _block_diag(w):
    per = GATE_BLK // HEAD_DIM
    w4 = w.reshape(D // GATE_BLK, per, HEAD_DIM, HEAD_DIM)
    out = jnp.zeros((D // GATE_BLK, GATE_BLK, GATE_BLK), w.dtype)
    for p in range(per):
        out = out.at[:, p * HEAD_DIM:(p + 1) * HEAD_DIM, p * HEAD_DIM:(p + 1) * HEAD_DIM].set(w4[:, p])
    return out


def _router_params(w_group, b_group, w_expert, b_expert):
    pad = LANES - N_GROUPS - N_EXPERTS
    wr = jnp.concatenate([w_group, w_expert, jnp.zeros((D, pad), F32)], axis=1).astype(BF16)
    br = jnp.concatenate([b_group, b_expert.reshape(-1), jnp.zeros((pad,), F32)])[None, :]
    return wr, br


def kernel(x, meta_tokens, lru_w_in, lru_conv_w, lru_conv_b, lru_w_a, lru_b_a, lru_w_i, lru_b_i, lru_lambda, lru_w_out, sc_w_in, sc_conv_w, sc_w_out, moe_w_group, moe_b_group, moe_w_expert, moe_b_expert, moe_w_gate, moe_w_up, moe_w_down, ln_g, ln_b):
    batch = x.shape[0]
    meta = jnp.broadcast_to(meta_tokens.astype(x.dtype)[None], (batch, N_META_TOK, D))
    h = jnp.concatenate([meta, x], axis=1)
    row = lambda v: v.reshape(1, -1)

    wr, br = _router_params(moe_w_group[0], moe_b_group[0], moe_w_expert[0], moe_b_expert[0])
    h1, route, cnt = _rglru_layer(
        h, lru_w_in[0].astype(BF16), lru_conv_w[0], row(lru_conv_b[0]),
        _block_diag(lru_w_a[0]).astype(BF16), _block_diag(lru_w_i[0]).astype(BF16),
        row(lru_b_a[0]), row(lru_b_i[0]), row(lru_lambda[0]), lru_w_out[0].astype(BF16),
        row(ln_g[0, 0]), row(ln_b[0, 0]), wr, br)
    h2 = _moe(h1.reshape(-1, D), route.reshape(-1, LANES), cnt,
              moe_w_gate[0].astype(BF16), moe_w_up[0].astype(BF16), moe_w_down[0].astype(BF16),
              row(ln_g[0, 1]), row(ln_b[0, 1]))

    wr, br = _router_params(moe_w_group[1], moe_b_group[1], moe_w_expert[1], moe_b_expert[1])
    h3, route, cnt = _shortconv_layer(
        h2.reshape(batch, SEQ_TOT, D), sc_w_in[0].astype(BF16), sc_conv_w[0], sc_w_out[0].astype(BF16),
        row(ln_g[1, 0]), row(ln_b[1, 0]), wr, br)
    h4 = _moe(h3.reshape(-1, D), route.reshape(-1, LANES), cnt,
              moe_w_gate[1].astype(BF16), moe_w_up[1].astype(BF16), moe_w_down[1].astype(BF16),
              row(ln_g[1, 1]), row(ln_b[1, 1]))
    return h4.reshape(batch, SEQ_TOT, D)[:, N_META_TOK:]
```

```python
import functools

import jax
import jax.numpy as jnp
from jax import lax
from jax.experimental import pallas as pl
from jax.experimental.pallas import tpu as pltpu

F32 = jnp.float32
BF16 = jnp.bfloat16

D = 1024
N_META_TOK = 16
SEQ_TOT = 8192 + N_META_TOK
LRU_HEADS = 16
HEAD_DIM = D // LRU_HEADS
LRU_C = 8.0
N_GROUPS = 4
EPG = 8
N_EXPERTS = N_GROUPS * EPG
D_EXPERT = D // 2
DEPTH = 2
ALPHA = (2.0 * DEPTH) ** 0.25
LN_EPS = 1e-5

LANES = 128
SUBLANES = 8
GATE_BLK = 256

TS = 432
TTD = 1368
TTC = 576
BM = 512
CH = 16
LC = 144
ISSUE_UNROLL = 8
NCHUNK = D // LANES


VMEM_LIMIT = 56 * 1024 * 1024


def _softplus(x):
    return jnp.maximum(x, 0.0) + jnp.log1p(jnp.exp(-jnp.abs(x)))


def _load_token_tiles(ref, r, n):
    return jnp.concatenate(
        [ref[pl.ds(r * SUBLANES + c, n, stride=SUBLANES), :] for c in range(NCHUNK)], axis=1)


def _store_token_tiles(ref, r, val):
    n = val.shape[0]
    for c in range(NCHUNK):
        ref[pl.ds(r * SUBLANES + c, n, stride=SUBLANES), :] = val[:, c * LANES:(c + 1) * LANES]


def _layer_norm_rows(z, g, bta):
    mu = jnp.mean(z, axis=-1, keepdims=True)
    zc = z - mu
    var = jnp.mean(zc * zc, axis=-1, keepdims=True)
    return zc * lax.rsqrt(var + LN_EPS) * g + bta


def _shifted(cur, prv, s, sub):
    return jnp.where(sub >= s, pltpu.roll(cur, s, 0), pltpu.roll(prv, s, 0))


def _ln_route_epilogue(h_ref, m_scr, lng_ref, lnb_ref, wr_ref, br_ref,
                       h1_ref, route_ref, cnt_ref, h1b_scr, tri_scr, cnt_scr, first):
    ts = m_scr.shape[0]

    @pl.when(first)
    def _():
        row = lax.broadcasted_iota(jnp.int32, (ts, ts), 0)
        col = lax.broadcasted_iota(jnp.int32, (ts, ts), 1)
        tri_scr[...] = jnp.where(col < row, 1.0, 0.0).astype(BF16)
        cnt_scr[...] = jnp.zeros_like(cnt_scr)

    g = lng_ref[...]
    bta = lnb_ref[...]

    h1_tiles = h1_ref.at[0]

    def ln_body(i, c):
        r = pl.multiple_of(i * LC, LC)
        z = ALPHA * h_ref[0, pl.ds(r, LC), :] + m_scr[pl.ds(r, LC), :]
        y = _layer_norm_rows(z, g, bta)
        _store_token_tiles(h1_tiles, r, y)
        h1b_scr[pl.ds(r, LC), :] = y.astype(BF16)
        return c

    lax.fori_loop(0, ts // LC, ln_body, 0)

    logits = jnp.dot(h1b_scr[...], wr_ref[...], preferred_element_type=F32) + br_ref[...]
    lane = lax.broadcasted_iota(jnp.int32, (ts, LANES), 1).astype(F32)
    neg = jnp.float32(-jnp.inf)
    big = jnp.float32(LANES)

    gmask = lane < N_GROUPS
    gl = jnp.where(gmask, logits, neg)
    gmax = jnp.max(gl, axis=-1, keepdims=True)
    gidx = jnp.min(jnp.where(gl == gmax, lane, big), axis=-1, keepdims=True)
    sumexp = jnp.sum(jnp.where(gmask, jnp.exp(gl - gmax), 0.0), axis=-1, keepdims=True)
    pg = 1.0 / sumexp

    lo = N_GROUPS + EPG * gidx
    emask = (lane >= lo) & (lane < lo + EPG)
    el = jnp.where(emask, logits, neg)
    v1 = jnp.max(el, axis=-1, keepdims=True)
    i1 = jnp.min(jnp.where(el == v1, lane, big), axis=-1, keepdims=True)
    el2 = jnp.where(lane == i1, neg, el)
    v2 = jnp.max(el2, axis=-1, keepdims=True)
    i2 = jnp.min(jnp.where(el2 == v2, lane, big), axis=-1, keepdims=True)
    ex = jnp.exp(v2 - v1)
    den = 1.0 / (1.0 + ex)
    g1 = pg * den
    g2 = pg * ex * den
    e1 = i1 - N_GROUPS
    e2 = i2 - N_GROUPS

    sel1 = lane == e1
    sel2 = lane == e2
    onehot = jnp.where(sel1 | sel2, 1.0, 0.0)
    before = jnp.dot(tri_scr[...], onehot.astype(BF16), preferred_element_type=F32) + cnt_scr[...]
    rank1 = jnp.sum(jnp.where(sel1, before, 0.0), axis=-1, keepdims=True)
    rank2 = jnp.sum(jnp.where(sel2, before, 0.0), axis=-1, keepdims=True)
    cnt_scr[...] = cnt_scr[...] + jnp.sum(onehot, axis=0, keepdims=True)

    slab = jnp.where(lane == 0, e1, 0.0)
    slab = jnp.where(lane == 1, e2, slab)
    slab = jnp.where(lane == 2, rank1, slab)
    slab = jnp.where(lane == 3, rank2, slab)
    slab = jnp.where(lane == 4, g1, slab)
    slab = jnp.where(lane == 5, g2, slab)
    route_ref[0] = slab
    cnt_ref[...] = cnt_scr[...]


def _rglru_kernel(h_ref, win_ref, cw_ref, cb_ref, wa_ref, wi_ref, ba_ref, bi_ref, lam_ref, wout_ref,
                  lng_ref, lnb_ref, wr_ref, br_ref,
                  h1_ref, route_ref, cnt_ref,
                  u_scr, xc_scr, xcb_scr, g_scr, y_scr, halo_scr, carry_scr, tri_scr, cnt_scr):
    b = pl.program_id(0)
    j = pl.program_id(1)
    ts = xc_scr.shape[0]

    @pl.when(j == 0)
    def _():
        halo_scr[...] = jnp.zeros_like(halo_scr)
        carry_scr[...] = jnp.zeros_like(carry_scr)

    u_scr[...] = jnp.dot(h_ref[0].astype(BF16), win_ref[...], preferred_element_type=F32)

    sub = lax.broadcasted_iota(jnp.int32, (SUBLANES, D), 0)
    cw = [cw_ref[k:k + 1, :] for k in range(4)]
    cb = cb_ref[...]

    def conv_body(i, prv):
        r = pl.multiple_of(i * CH, CH)
        outs = []
        for hh in range(CH // SUBLANES):
            cur = u_scr[pl.ds(r + SUBLANES * hh, SUBLANES), D:]
            acc = cur * cw[3] + cb
            for s in (1, 2, 3):
                acc = acc + _shifted(cur, prv, s, sub) * cw[3 - s]
            outs.append(acc)
            prv = cur
        xc = jnp.concatenate(outs, axis=0)
        xc_scr[pl.ds(r, CH), :] = xc
        xcb_scr[pl.ds(r, CH), :] = xc.astype(BF16)
        return prv

    halo_scr[...] = lax.fori_loop(0, ts // CH, conv_body, halo_scr[...])

    for k in range(D // GATE_BLK):
        xk = xcb_scr[:, k * GATE_BLK:(k + 1) * GATE_BLK]
        g_scr[:, k * GATE_BLK:(k + 1) * GATE_BLK] = jnp.dot(xk, wa_ref[k], preferred_element_type=F32)
        g_scr[:, D + k * GATE_BLK:D + (k + 1) * GATE_BLK] = jnp.dot(xk, wi_ref[k], preferred_element_type=F32)

    ba = ba_ref[...]
    bi = bi_ref[...]
    nsp = -LRU_C * _softplus(-lam_ref[...])

    def scan_body(i, c):
        r = pl.multiple_of(i * CH, CH)
        ys = []
        for hh in range(CH // SUBLANES):
            r0 = r + SUBLANES * hh
            xc = xc_scr[pl.ds(r0, SUBLANES), :]
            ra = jax.nn.sigmoid(g_scr[pl.ds(r0, SUBLANES), :D] + ba)
            gi = jax.nn.sigmoid(g_scr[pl.ds(r0, SUBLANES), D:] + bi)
            log_a = ra * nsp
            a = jnp.exp(log_a)
            m2 = -jnp.tanh(log_a) * (a * a + 1.0)
            bv = jnp.sqrt(m2) * (gi * xc)
            av = a
            for s in (1, 2, 4):
                m = sub >= s
                a_s = pltpu.roll(av, s, 0)
                b_s = pltpu.roll(bv, s, 0)
                bv = jnp.where(m, av * b_s + bv, bv)
                av = jnp.where(m, av * a_s, av)
            hs = av * c + bv
            c = jnp.broadcast_to(hs[SUBLANES - 1:SUBLANES, :], (SUBLANES, D))
            ys.append(hs * jax.nn.gelu(u_scr[pl.ds(r0, SUBLANES), :D]))
        y_scr[pl.ds(r, CH), :] = jnp.concatenate(ys, axis=0).astype(BF16)
        return c

    carry_scr[...] = lax.fori_loop(0, ts // CH, scan_body, carry_scr[...])

    xc_scr[...] = jnp.dot(y_scr[...], wout_ref[...], preferred_element_type=F32)
    _ln_route_epilogue(h_ref, xc_scr, lng_ref, lnb_ref, wr_ref, br_ref,
                       h1_ref, route_ref, cnt_ref, xcb_scr, tri_scr, cnt_scr,
                       (b == 0) & (j == 0))


def _shortconv_kernel(h_ref, win_ref, cw_ref, wout_ref, lng_ref, lnb_ref, wr_ref, br_ref,
                      h1_ref, route_ref, cnt_ref,
                      u_scr, m_scr, y_scr, halo_scr, tri_scr, cnt_scr):
    b = pl.program_id(0)
    j = pl.program_id(1)
    ts = m_scr.shape[0]

    @pl.when(j == 0)
    def _():
        halo_scr[...] = jnp.zeros_like(halo_scr)

    u_scr[...] = jnp.dot(h_ref[0].astype(BF16), win_ref[...], preferred_element_type=F32)

    sub = lax.broadcasted_iota(jnp.int32, (SUBLANES, D), 0)
    cw = [cw_ref[k:k + 1, :] for k in range(3)]

    def conv_body(i, prv):
        r = pl.multiple_of(i * CH, CH)
        ys = []
        for hh in range(CH // SUBLANES):
            r0 = r + SUBLANES * hh
            cur = u_scr[pl.ds(r0, SUBLANES), D:2 * D] * u_scr[pl.ds(r0, SUBLANES), 2 * D:]
            acc = cur * cw[2]
            for s in (1, 2):
                acc = acc + _shifted(cur, prv, s, sub) * cw[2 - s]
            ys.append(u_scr[pl.ds(r0, SUBLANES), :D] * acc)
            prv = cur
        y_scr[pl.ds(r, CH), :] = jnp.concatenate(ys, axis=0).astype(BF16)
        return prv

    halo_scr[...] = lax.fori_loop(0, ts // CH, conv_body, halo_scr[...])

    m_scr[...] = jnp.dot(y_scr[...], wout_ref[...], preferred_element_type=F32)
    _ln_route_epilogue(h_ref, m_scr, lng_ref, lnb_ref, wr_ref, br_ref,
                       h1_ref, route_ref, cnt_ref, y_scr, tri_scr, cnt_scr,
                       (b == 0) & (j == 0))


def _const_spec(shape):
    n = len(shape)
    return pl.BlockSpec(shape, lambda b, j: (0,) * n)


def _mixer_out(batch):
    nj = SEQ_TOT // TS
    out_shape = (jax.ShapeDtypeStruct((batch, SEQ_TOT * SUBLANES, LANES), F32),
                 jax.ShapeDtypeStruct((batch, SEQ_TOT, LANES), F32),
                 jax.ShapeDtypeStruct((1, LANES), F32))
    out_specs = (pl.BlockSpec((1, TS * SUBLANES, LANES), lambda b, j: (b, j, 0)),
                 pl.BlockSpec((1, TS, LANES), lambda b, j: (b, j, 0)),
                 pl.BlockSpec((1, LANES), lambda b, j: (0, 0)))
    return nj, out_shape, out_specs


def _rglru_layer(h, win, cw, cb, wa, wi, ba, bi, lam, wout, lng, lnb, wr, br):
    batch = h.shape[0]
    nj, out_shape, out_specs = _mixer_out(batch)
    consts = (win, cw, cb, wa, wi, ba, bi, lam, wout, lng, lnb, wr, br)
    return pl.pallas_call(
        _rglru_kernel,
        grid=(batch, nj),
        in_specs=[pl.BlockSpec((1, TS, D), lambda b, j: (b, j, 0))] + [_const_spec(c.shape) for c in consts],
        out_specs=out_specs,
        out_shape=out_shape,
        scratch_shapes=[
            pltpu.VMEM((TS, 2 * D), F32),
            pltpu.VMEM((TS, D), F32),
            pltpu.VMEM((TS, D), BF16),
            pltpu.VMEM((TS, 2 * D), F32),
            pltpu.VMEM((TS, D), BF16),
            pltpu.VMEM((SUBLANES, D), F32),
            pltpu.VMEM((SUBLANES, D), F32),
            pltpu.VMEM((TS, TS), BF16),
            pltpu.VMEM((1, LANES), F32),
        ],
        compiler_params=pltpu.CompilerParams(
            dimension_semantics=("arbitrary", "arbitrary"), vmem_limit_bytes=VMEM_LIMIT),
        name="rglru_layer",
    )(h, *consts)


def _shortconv_layer(h, win, cw, wout, lng, lnb, wr, br):
    batch = h.shape[0]
    nj, out_shape, out_specs = _mixer_out(batch)
    consts = (win, cw, wout, lng, lnb, wr, br)
    return pl.pallas_call(
        _shortconv_kernel,
        grid=(batch, nj),
        in_specs=[pl.BlockSpec((1, TS, D), lambda b, j: (b, j, 0))] + [_const_spec(c.shape) for c in consts],
        out_specs=out_specs,
        out_shape=out_shape,
        scratch_shapes=[
            pltpu.VMEM((TS, 3 * D), F32),
            pltpu.VMEM((TS, D), F32),
            pltpu.VMEM((TS, D), BF16),
            pltpu.VMEM((SUBLANES, D), F32),
            pltpu.VMEM((TS, TS), BF16),
            pltpu.VMEM((1, LANES), F32),
        ],
        compiler_params=pltpu.CompilerParams(
            dimension_semantics=("arbitrary", "arbitrary"), vmem_limit_bytes=VMEM_LIMIT),
        name="shortconv_layer",
    )(h, *consts)


def _tile_rows(t):
    return pl.ds(pl.multiple_of(t * SUBLANES, SUBLANES), SUBLANES)


def _dispatch_kernel(pend_ref, padded_ref, nused_ref, dest_ref, h_ref, xs_ref, zero_scr, sem_z, sem):
    i = pl.program_id(0)
    tt = h_ref.shape[0] // SUBLANES
    n_blocks = xs_ref.shape[0] // (BM * SUBLANES)

    def zero_block(blk):
        start = pl.multiple_of(blk * (BM * SUBLANES), BM * SUBLANES)
        return pltpu.make_async_copy(zero_scr, xs_ref.at[pl.ds(start, BM * SUBLANES), :], sem_z)

    @pl.when(i == 0)
    def _():
        zero_scr[...] = jnp.zeros_like(zero_scr)
        for e in range(N_EXPERTS):
            @pl.when(padded_ref[e] > 0)
            def _():
                zero_block(pend_ref[e] // BM - 1).start()

        def fill_tail(blk, c):
            zero_block(blk).start()
            return c

        lax.fori_loop(nused_ref[0], n_blocks, fill_tail, 0)
        for e in range(N_EXPERTS):
            @pl.when(padded_ref[e] > 0)
            def _():
                zero_block(0).wait()

        def drain_tail(blk, c):
            zero_block(0).wait()
            return c

        lax.fori_loop(nused_ref[0], n_blocks, drain_tail, 0)

    def row_copy(t, d):
        return pltpu.make_async_copy(h_ref.at[_tile_rows(t), :], xs_ref.at[_tile_rows(d), :], sem)

    def issue(i8, c):
        for u in range(ISSUE_UNROLL):
            t = i8 * ISSUE_UNROLL + u
            for k in range(2):
                row_copy(t, dest_ref[0, 0, 2 * t + k]).start(priority=k)
        return c

    lax.fori_loop(0, tt // ISSUE_UNROLL, issue, 0)

    def drain(i8, c):
        for _ in range(2 * ISSUE_UNROLL):
            row_copy(0, 0).wait()
        return c

    lax.fori_loop(0, tt // ISSUE_UNROLL, drain, 0)


def _dispatch(h1t, dest, pend, padded, n_used, n_rows):
    t_tot = h1t.shape[0] // SUBLANES
    nt = t_tot // TTD
    return pl.pallas_call(
        _dispatch_kernel,
        grid_spec=pltpu.PrefetchScalarGridSpec(
            num_scalar_prefetch=3,
            grid=(nt,),
            in_specs=[pl.BlockSpec((1, 1, 2 * TTD), lambda i, *_: (i, 0, 0), memory_space=pltpu.SMEM),
                      pl.BlockSpec((TTD * SUBLANES, LANES), lambda i, *_: (i, 0))],
            out_specs=pl.BlockSpec(memory_space=pl.ANY),
            scratch_shapes=[pltpu.VMEM((BM * SUBLANES, LANES), F32),
                            pltpu.SemaphoreType.DMA(()),
                            pltpu.SemaphoreType.DMA(())],
        ),
        out_shape=jax.ShapeDtypeStruct((n_rows * SUBLANES, LANES), F32),
        compiler_params=pltpu.CompilerParams(dimension_semantics=("arbitrary",)),
        name="moe_dispatch",
    )(pend, padded, n_used, dest.reshape(nt, 1, 2 * TTD), h1t)


def _expert_kernel(be_ref, nu_ref, xs_ref, wg_ref, wu_ref, wd_ref, ys_ref, wg_b, wu_b, wd_b):
    i = pl.program_id(0)

    @pl.when(i < nu_ref[0])
    def _():
        @pl.when((i == 0) | (be_ref[i] != be_ref[jnp.maximum(i - 1, 0)]))
        def _():
            wg_b[...] = wg_ref[0, 0].astype(BF16)
            wu_b[...] = wu_ref[0, 0].astype(BF16)
            wd_b[...] = wd_ref[0, 0].astype(BF16)

        xb = _load_token_tiles(xs_ref, 0, BM).astype(BF16)
        g = jnp.dot(xb, wg_b[...], preferred_element_type=F32)
        u = jnp.dot(xb, wu_b[...], preferred_element_type=F32)
        hmid = (jax.nn.silu(g) * u).astype(BF16)
        _store_token_tiles(ys_ref, 0, jnp.dot(hmid, wd_b[...], preferred_element_type=F32))

    @pl.when(i >= nu_ref[0])
    def _():
        ys_ref[...] = jnp.zeros_like(ys_ref)


def _experts(xs, block_expert, n_used, wg, wu, wd, layer):
    blk_rows = BM * SUBLANES
    nb = xs.shape[0] // blk_rows
    return pl.pallas_call(
        _expert_kernel,
        grid_spec=pltpu.PrefetchScalarGridSpec(
            num_scalar_prefetch=2,
            grid=(nb,),
            in_specs=[pl.BlockSpec((blk_rows, LANES), lambda i, be, nu: (jnp.minimum(i, nu[0] - 1), 0)),
                      pl.BlockSpec((1, 1, D, D_EXPERT), lambda i, be, nu: (layer, be[i], 0, 0)),
                      pl.BlockSpec((1, 1, D, D_EXPERT), lambda i, be, nu: (layer, be[i], 0, 0)),
                      pl.BlockSpec((1, 1, D_EXPERT, D), lambda i, be, nu: (layer, be[i], 0, 0))],
            out_specs=pl.BlockSpec((blk_rows, LANES), lambda i, be, nu: (i, 0)),
            scratch_shapes=[pltpu.VMEM((D, D_EXPERT), BF16),
                            pltpu.VMEM((D, D_EXPERT), BF16),
                            pltpu.VMEM((D_EXPERT, D), BF16)],
        ),
        out_shape=jax.ShapeDtypeStruct(xs.shape, F32),
        compiler_params=pltpu.CompilerParams(
            dimension_semantics=("arbitrary",), vmem_limit_bytes=VMEM_LIMIT),
        name="moe_experts",
    )(block_expert, n_used, xs, wg, wu, wd)


def _combine_kernel(dest_ref, h_ref, route_ref, lng_ref, lnb_ref, ys_ref, out_ref, y0_scr, y1_scr, sems):
    tt = out_ref.shape[0]
    half = tt // 2

    def row_copy(d, buf, t, sem):
        return pltpu.make_async_copy(ys_ref.at[_tile_rows(d), :], buf.at[_tile_rows(t), :], sem)

    for hf in range(2):
        def issue(i8, c, hf=hf):
            for u in range(ISSUE_UNROLL):
                t = hf * half + i8 * ISSUE_UNROLL + u
                row_copy(dest_ref[0, 0, 2 * t], y0_scr, t, sems.at[hf]).start(priority=0)
                row_copy(dest_ref[0, 0, 2 * t + 1], y1_scr, t, sems.at[hf]).start(priority=1)
            return c

        lax.fori_loop(0, half // ISSUE_UNROLL, issue, 0)

    g = lng_ref[...]
    bta = lnb_ref[...]

    for hf in range(2):
        def drain(i8, c, hf=hf):
            for _ in range(2 * ISSUE_UNROLL):
                row_copy(0, y0_scr, 0, sems.at[hf]).wait()
            return c

        lax.fori_loop(0, half // ISSUE_UNROLL, drain, 0)

        for r in range(hf * half, (hf + 1) * half, LC):
            g1 = route_ref[r:r + LC, 4:5]
            g2 = route_ref[r:r + LC, 5:6]
            y = g1 * _load_token_tiles(y0_scr, r, LC) + g2 * _load_token_tiles(y1_scr, r, LC)
            z = ALPHA * _load_token_tiles(h_ref, r, LC) + y
            out_ref[r:r + LC, :] = _layer_norm_rows(z, g, bta)


def _combine(h1t, route, dest, ys, lng, lnb):
    t_tot = h1t.shape[0] // SUBLANES
    nt = t_tot // TTC
    return pl.pallas_call(
        _combine_kernel,
        grid=(nt,),
        in_specs=[pl.BlockSpec((1, 1, 2 * TTC), lambda i: (i, 0, 0), memory_space=pltpu.SMEM),
                  pl.BlockSpec((TTC * SUBLANES, LANES), lambda i: (i, 0)),
                  pl.BlockSpec((TTC, LANES), lambda i: (i, 0)),
                  pl.BlockSpec((1, D), lambda i: (0, 0)),
                  pl.BlockSpec((1, D), lambda i: (0, 0)),
                  pl.BlockSpec(memory_space=pl.ANY)],
        out_specs=pl.BlockSpec((TTC, D), lambda i: (i, 0)),
        out_shape=jax.ShapeDtypeStruct((t_tot, D), F32),
        scratch_shapes=[pltpu.VMEM((TTC * SUBLANES, LANES), F32),
                        pltpu.VMEM((TTC * SUBLANES, LANES), F32),
                        pltpu.SemaphoreType.DMA((2,))],
        compiler_params=pltpu.CompilerParams(dimension_semantics=("arbitrary",)),
        name="moe_combine",
    )(dest.reshape(nt, 1, 2 * TTC), h1t, route, lng, lnb, ys)


def _moe(h1t, route, cnt, wg, wu, wd, layer, lng, lnb):
    t_tot = route.shape[0]
    n_blocks = (2 * t_tot + N_EXPERTS * (BM - 1) + BM - 1) // BM
    eid = route[:, 0:2].astype(jnp.int32)
    rank = route[:, 2:4].astype(jnp.int32)
    counts = cnt[0, :N_EXPERTS].astype(jnp.int32)
    padded = (counts + BM - 1) // BM * BM
    pend = jnp.cumsum(padded)
    pstart = pend - padded
    dest = (pstart[eid] + rank).reshape(-1)
    n_used = pend[-1:] // BM
    blk = jnp.minimum(jnp.arange(n_blocks, dtype=jnp.int32), n_used[0] - 1) * BM
    block_expert = jnp.minimum(jnp.sum((pend[None, :] <= blk[:, None]).astype(jnp.int32), axis=1),
                               N_EXPERTS - 1)

    xs = _dispatch(h1t, dest, pend, padded, n_used, n_blocks * BM)
    ys = _experts(xs, block_expert, n_used, wg, wu, wd, layer)
    return _combine(h1t, route, dest, ys, lng, lnb)


def _block_diag(w):
    per = GATE_BLK // HEAD_DIM
    w4 = w.reshape(D // GATE_BLK, per, HEAD_DIM, HEAD_DIM)
    out = jnp.zeros((D // GATE_BLK, GATE_BLK, GATE_BLK), w.dtype)
    for p in range(per):
        out = out.at[:, p * HEAD_DIM:(p + 1) * HEAD_DIM, p * HEAD_DIM:(p + 1) * HEAD_DIM].set(w4[:, p])
    return out


def _router_params(w_group, b_group, w_expert, b_expert):
    pad = LANES - N_GROUPS - N_EXPERTS
    wr = jnp.concatenate([w_group, w_expert, jnp.zeros((D, pad), F32)], axis=1).astype(BF16)
    br = jnp.concatenate([b_group, b_expert.reshape(-1), jnp.zeros((pad,), F32)])[None, :]
    return wr, br


def kernel(x, meta_tokens, lru_w_in, lru_conv_w, lru_conv_b, lru_w_a, lru_b_a, lru_w_i, lru_b_i, lru_lambda, lru_w_out, sc_w_in, sc_conv_w, sc_w_out, moe_w_group, moe_b_group, moe_w_expert, moe_b_expert, moe_w_gate, moe_w_up, moe_w_down, ln_g, ln_b):
    batch = x.shape[0]
    meta = jnp.broadcast_to(meta_tokens.astype(x.dtype)[None], (batch, N_META_TOK, D))
    h = jnp.concatenate([meta, x], axis=1)
    row = lambda v: v.reshape(1, -1)

    wr, br = _router_params(moe_w_group[0], moe_b_group[0], moe_w_expert[0], moe_b_expert[0])
    h1, route, cnt = _rglru_layer(
        h, lru_w_in[0].astype(BF16), lru_conv_w[0], row(lru_conv_b[0]),
        _block_diag(lru_w_a[0]).astype(BF16), _block_diag(lru_w_i[0]).astype(BF16),
        row(lru_b_a[0]), row(lru_b_i[0]), row(lru_lambda[0]), lru_w_out[0].astype(BF16),
        row(ln_g[0, 0]), row(ln_b[0, 0]), wr, br)
    h2 = _moe(h1.reshape(-1, LANES), route.reshape(-1, LANES), cnt,
              moe_w_gate, moe_w_up, moe_w_down, 0, row(ln_g[0, 1]), row(ln_b[0, 1]))

    wr, br = _router_params(moe_w_group[1], moe_b_group[1], moe_w_expert[1], moe_b_expert[1])
    h3, route, cnt = _shortconv_layer(
        h2.reshape(batch, SEQ_TOT, D), sc_w_in[0].astype(BF16), sc_conv_w[0], sc_w_out[0].astype(BF16),
        row(ln_g[1, 0]), row(ln_b[1, 0]), wr, br)
    h4 = _moe(h3.reshape(-1, LANES), route.reshape(-1, LANES), cnt,
              moe_w_gate, moe_w_up, moe_w_down, 1, row(ln_g[1, 1]), row(ln_b[1, 1]))
    return h4.reshape(batch, SEQ_TOT, D)[:, N_META_TOK:]
```

```python
import functools

import jax
import jax.numpy as jnp
from jax import lax
from jax.experimental import pallas as pl
from jax.experimental.pallas import tpu as pltpu

F32 = jnp.float32
BF16 = jnp.bfloat16

D = 1024
SEQ = 8192
N_META_TOK = 16
SEQ_TOT = SEQ + N_META_TOK
LRU_HEADS = 16
HEAD_DIM = D // LRU_HEADS
LRU_C = 8.0
N_GROUPS = 4
EPG = 8
N_EXPERTS = N_GROUPS * EPG
D_EXPERT = D // 2
DEPTH = 2
ALPHA = (2.0 * DEPTH) ** 0.25
LN_EPS = 1e-5

LANES = 128
SUBLANES = 8
GATE_BLK = 256

TS = 432
SEG = TS // SUBLANES
TTM = 2 * TS
TF = 512
BM = 512
CH = 16
LC = 144
LCF = 128
WALK_UNROLL = 6
ISSUE_UNROLL = 8
NCHUNK = D // LANES

VMEM_LIMIT = 56 * 1024 * 1024


def _softplus(x):
    return jnp.maximum(x, 0.0) + jnp.log1p(jnp.exp(-jnp.abs(x)))


def _sigmoid(x):
    return 0.5 * jnp.tanh(0.5 * x) + 0.5


def _load_token_tiles(ref, r, n):
    return jnp.concatenate(
        [ref[pl.ds(r * SUBLANES + c, n, stride=SUBLANES), :] for c in range(NCHUNK)], axis=1)


def _store_token_tiles(ref, r, val):
    n = val.shape[0]
    for c in range(NCHUNK):
        ref[pl.ds(r * SUBLANES + c, n, stride=SUBLANES), :] = val[:, c * LANES:(c + 1) * LANES]


def _layer_norm_rows(z, g, bta):
    mu = jnp.mean(z, axis=-1, keepdims=True)
    zc = z - mu
    var = jnp.mean(zc * zc, axis=-1, keepdims=True)
    return zc * lax.rsqrt(var + LN_EPS) * g + bta


def _shifted(cur, prv, s, sub):
    return jnp.where(sub >= s, pltpu.roll(cur, s, 0), pltpu.roll(prv, s, 0))


def _exact_transpose_8(slab):
    sel = jnp.where(lax.broadcasted_iota(jnp.int32, (SUBLANES, LANES), 0)
                    == lax.broadcasted_iota(jnp.int32, (SUBLANES, LANES), 1), 1.0, 0.0).astype(BF16)
    p1 = slab.astype(BF16)
    r1 = slab - p1.astype(F32)
    p2 = r1.astype(BF16)
    p3 = (r1 - p2.astype(F32)).astype(BF16)
    dn = (((1,), (1,)), ((), ()))
    out = lax.dot_general(sel, p1, dn, preferred_element_type=F32)
    out = out + lax.dot_general(sel, p2, dn, preferred_element_type=F32)
    return out + lax.dot_general(sel, p3, dn, preferred_element_type=F32)


def _ln_route_epilogue(h2d, m_scr, lng_ref, lnb_ref, wr_ref, br_ref,
                       h1_ref, route_ref, rt_ref, cnt_ref, h1b_scr, tri_scr, cnt_scr, first):
    ts = m_scr.shape[0]

    @pl.when(first)
    def _():
        row = lax.broadcasted_iota(jnp.int32, (ts, ts), 0)
        col = lax.broadcasted_iota(jnp.int32, (ts, ts), 1)
        tri_scr[...] = jnp.where(col < row, 1.0, 0.0).astype(BF16)
        cnt_scr[...] = jnp.zeros_like(cnt_scr)

    g = lng_ref[...]
    bta = lnb_ref[...]

    h1_tiles = h1_ref.at[0]

    def ln_body(i, c):
        r = pl.multiple_of(i * LC, LC)
        z = ALPHA * h2d[pl.ds(r, LC), :] + m_scr[pl.ds(r, LC), :]
        y = _layer_norm_rows(z, g, bta)
        _store_token_tiles(h1_tiles, r, y)
        h1b_scr[pl.ds(r, LC), :] = y.astype(BF16)
        return c

    lax.fori_loop(0, ts // LC, ln_body, 0)

    logits = jnp.dot(h1b_scr[...], wr_ref[...], preferred_element_type=F32) + br_ref[...]
    lane = lax.broadcasted_iota(jnp.int32, (ts, LANES), 1).astype(F32)
    neg = jnp.float32(-jnp.inf)
    big = jnp.float32(LANES)

    gmask = lane < N_GROUPS
    gl = jnp.where(gmask, logits, neg)
    gmax = jnp.max(gl, axis=-1, keepdims=True)
    gidx = jnp.min(jnp.where(gl == gmax, lane, big), axis=-1, keepdims=True)
    sumexp = jnp.sum(jnp.where(gmask, jnp.exp(gl - gmax), 0.0), axis=-1, keepdims=True)
    pg = 1.0 / sumexp

    lo = N_GROUPS + EPG * gidx
    emask = (lane >= lo) & (lane < lo + EPG)
    el = jnp.where(emask, logits, neg)
    v1 = jnp.max(el, axis=-1, keepdims=True)
    i1 = jnp.min(jnp.where(el == v1, lane, big), axis=-1, keepdims=True)
    el2 = jnp.where(lane == i1, neg, el)
    v2 = jnp.max(el2, axis=-1, keepdims=True)
    i2 = jnp.min(jnp.where(el2 == v2, lane, big), axis=-1, keepdims=True)
    ex = jnp.exp(v2 - v1)
    den = 1.0 / (1.0 + ex)
    g1 = pg * den
    g2 = pg * ex * den
    e1 = i1 - N_GROUPS
    e2 = i2 - N_GROUPS

    sel1 = lane == e1
    sel2 = lane == e2
    onehot = jnp.where(sel1 | sel2, 1.0, 0.0)
    before = jnp.dot(tri_scr[...], onehot.astype(BF16), preferred_element_type=F32) + cnt_scr[...]
    rank1 = jnp.sum(jnp.where(sel1, before, 0.0), axis=-1, keepdims=True)
    rank2 = jnp.sum(jnp.where(sel2, before, 0.0), axis=-1, keepdims=True)
    cnt_scr[...] = cnt_scr[...] + jnp.sum(onehot, axis=0, keepdims=True)

    slab = jnp.where(lane == 0, e1, 0.0)
    slab = jnp.where(lane == 1, e2, slab)
    slab = jnp.where(lane == 2, rank1, slab)
    slab = jnp.where(lane == 3, rank2, slab)
    slab = jnp.where(lane == 4, g1, slab)
    slab = jnp.where(lane == 5, g2, slab)
    route_ref[0] = slab
    rt_ref[0, 0] = _exact_transpose_8(slab)
    cnt_ref[...] = cnt_scr[...]


def _seg_load(ref3, c0, g):
    return jnp.concatenate(
        [ref3[c0 + c, pl.ds(g, SUBLANES, stride=SEG), :] for c in range(NCHUNK)], axis=1)


def _seg_store(ref3, g, val):
    for c in range(NCHUNK):
        ref3[c, pl.ds(g, SUBLANES, stride=SEG), :] = val[:, c * LANES:(c + 1) * LANES]


def _rglru_kernel(x_ref, meta_ref, win_ref, cw_ref, cb_ref, wa_ref, wi_ref, ba_ref, bi_ref, lam_ref,
                  wout_ref, lng_ref, lnb_ref, wr_ref, br_ref,
                  h1_ref, route_ref, rt_ref, cnt_ref,
                  h_scr, u_scr, xc_scr, xcb_scr, g_scr, y_scr, halo_scr, carry_scr, tri_scr, cnt_scr):
    b = pl.program_id(0)
    j = pl.program_id(1)

    @pl.when(j == 0)
    def _():
        halo_scr[...] = jnp.zeros_like(halo_scr)
        carry_scr[...] = jnp.zeros_like(carry_scr)
        h_scr[0:N_META_TOK, :] = meta_ref[...]
        h_scr[N_META_TOK:TS, :] = x_ref[0, 0:TS - N_META_TOK, :]

    @pl.when(j > 0)
    def _():
        h_scr[...] = x_ref[0]

    u = jnp.dot(h_scr[...].astype(BF16), win_ref[...], preferred_element_type=F32)
    for c in range(2 * NCHUNK):
        u_scr[c] = u[:, c * LANES:(c + 1) * LANES]

    sub = lax.broadcasted_iota(jnp.int32, (SUBLANES, D), 0)
    cw = [cw_ref[k:k + 1, :] for k in range(4)]
    cb = cb_ref[...]

    tails = []
    for m in range(3):
        um = _seg_load(u_scr, NCHUNK, SEG - 3 + m)
        tails.append(jnp.where(sub >= 1, pltpu.roll(um, 1, 0),
                               pltpu.roll(halo_scr[m * SUBLANES:(m + 1) * SUBLANES, :], 1, 0)))
        halo_scr[m * SUBLANES:(m + 1) * SUBLANES, :] = um

    def conv_body(i, carry):
        x1, x2, x3 = carry
        outs = []
        for hh in range(CH // SUBLANES):
            cur = _seg_load(u_scr, NCHUNK, 2 * i + hh)
            outs.append(cur * cw[3] + x1 * cw[2] + x2 * cw[1] + x3 * cw[0] + cb)
            x1, x2, x3 = cur, x1, x2
        xc = jnp.concatenate(outs, axis=0)
        r = pl.multiple_of(i * CH, CH)
        xc_scr[pl.ds(r, CH), :] = xc
        xcb_scr[pl.ds(r, CH), :] = xc.astype(BF16)
        return x1, x2, x3

    lax.fori_loop(0, TS // CH, conv_body, (tails[2], tails[1], tails[0]))

    for k in range(D // GATE_BLK):
        xk = xcb_scr[:, k * GATE_BLK:(k + 1) * GATE_BLK]
        g_scr[:, k * GATE_BLK:(k + 1) * GATE_BLK] = jnp.dot(xk, wa_ref[k], preferred_element_type=F32)
        g_scr[:, D + k * GATE_BLK:D + (k + 1) * GATE_BLK] = jnp.dot(xk, wi_ref[k], preferred_element_type=F32)

    ba = ba_ref[...]
    bi = bi_ref[...]
    nsp = -LRU_C * _softplus(-lam_ref[...])

    def scan_body(i, carry):
        hl, pc = carry
        for hh in range(WALK_UNROLL):
            rows = pl.ds(pl.multiple_of((i * WALK_UNROLL + hh) * SUBLANES, SUBLANES), SUBLANES)
            xc = xc_scr[rows, :]
            ra = _sigmoid(g_scr[rows, :D] + ba)
            gi = _sigmoid(g_scr[rows, D:] + bi)
            log_a = ra * nsp
            a = jnp.exp(log_a)
            m2 = -jnp.tanh(log_a) * (a * a + 1.0)
            hl = a * hl + jnp.sqrt(m2) * (gi * xc)
            pc = a * pc
            g_scr[rows, :D] = hl
            g_scr[rows, D:] = pc
        return hl, pc

    hl_end, pc_end = lax.fori_loop(0, SEG // WALK_UNROLL, scan_body,
                                   (jnp.zeros((SUBLANES, D), F32), jnp.ones((SUBLANES, D), F32)))

    c0 = carry_scr[...]
    av, bv = pc_end, hl_end
    for s in (1, 2, 4):
        m = sub >= s
        a_s = pltpu.roll(av, s, 0)
        b_s = pltpu.roll(bv, s, 0)
        bv = jnp.where(m, av * b_s + bv, bv)
        av = jnp.where(m, av * a_s, av)
    h_end = av * c0 + bv
    cin = jnp.where(sub >= 1, pltpu.roll(h_end, 1, 0), c0)
    carry_scr[...] = jnp.broadcast_to(h_end[SUBLANES - 1:SUBLANES, :], (SUBLANES, D))

    def fix_body(i, c):
        for hh in range(WALK_UNROLL):
            g = i * WALK_UNROLL + hh
            rows = pl.ds(pl.multiple_of(g * SUBLANES, SUBLANES), SUBLANES)
            hs = g_scr[rows, :D] + g_scr[rows, D:] * cin
            _seg_store(y_scr, g, hs * jax.nn.gelu(_seg_load(u_scr, 0, g)))
        return c

    lax.fori_loop(0, SEG // WALK_UNROLL, fix_body, 0)

    y = jnp.concatenate([y_scr[c] for c in range(NCHUNK)], axis=1).astype(BF16)
    xc_scr[...] = jnp.dot(y, wout_ref[...], preferred_element_type=F32)
    _ln_route_epilogue(h_scr, xc_scr, lng_ref, lnb_ref, wr_ref, br_ref,
                       h1_ref, route_ref, rt_ref, cnt_ref, xcb_scr, tri_scr, cnt_scr,
                       (b == 0) & (j == 0))


def _shortconv_kernel(h_ref, win_ref, cw_ref, wout_ref, lng_ref, lnb_ref, wr_ref, br_ref,
                      h1_ref, route_ref, rt_ref, cnt_ref,
                      u_scr, m_scr, y_scr, halo_scr, tri_scr, cnt_scr):
    b = pl.program_id(0)
    j = pl.program_id(1)
    ts = m_scr.shape[0]

    @pl.when(j == 0)
    def _():
        halo_scr[...] = jnp.zeros_like(halo_scr)

    u_scr[...] = jnp.dot(h_ref[0].astype(BF16), win_ref[...], preferred_element_type=F32)

    sub = lax.broadcasted_iota(jnp.int32, (SUBLANES, D), 0)
    cw = [cw_ref[k:k + 1, :] for k in range(3)]

    def conv_body(i, prv):
        r = pl.multiple_of(i * CH, CH)
        ys = []
        for hh in range(CH // SUBLANES):
            r0 = r + SUBLANES * hh
            cur = u_scr[pl.ds(r0, SUBLANES), D:2 * D] * u_scr[pl.ds(r0, SUBLANES), 2 * D:]
            acc = cur * cw[2]
            for s in (1, 2):
                acc = acc + _shifted(cur, prv, s, sub) * cw[2 - s]
            ys.append(u_scr[pl.ds(r0, SUBLANES), :D] * acc)
            prv = cur
        y_scr[pl.ds(r, CH), :] = jnp.concatenate(ys, axis=0).astype(BF16)
        return prv

    halo_scr[...] = lax.fori_loop(0, ts // CH, conv_body, halo_scr[...])

    m_scr[...] = jnp.dot(y_scr[...], wout_ref[...], preferred_element_type=F32)
    _ln_route_epilogue(h_ref.at[0], m_scr, lng_ref, lnb_ref, wr_ref, br_ref,
                       h1_ref, route_ref, rt_ref, cnt_ref, y_scr, tri_scr, cnt_scr,
                       (b == 0) & (j == 0))


def _const_spec(shape):
    n = len(shape)
    return pl.BlockSpec(shape, lambda b, j: (0,) * n)


def _mixer_out(batch):
    nj = SEQ_TOT // TS
    out_shape = (jax.ShapeDtypeStruct((batch, SEQ_TOT * SUBLANES, LANES), F32),
                 jax.ShapeDtypeStruct((batch, SEQ_TOT, LANES), F32),
                 jax.ShapeDtypeStruct((batch, nj, SUBLANES, TS), F32),
                 jax.ShapeDtypeStruct((1, LANES), F32))
    out_specs = (pl.BlockSpec((1, TS * SUBLANES, LANES), lambda b, j: (b, j, 0)),
                 pl.BlockSpec((1, TS, LANES), lambda b, j: (b, j, 0)),
                 pl.BlockSpec((1, 1, SUBLANES, TS), lambda b, j: (b, j, 0, 0)),
                 pl.BlockSpec((1, LANES), lambda b, j: (0, 0)))
    return nj, out_shape, out_specs


def _rglru_layer(x, meta, win, cw, cb, wa, wi, ba, bi, lam, wout, lng, lnb, wr, br):
    batch = x.shape[0]
    nj, out_shape, out_specs = _mixer_out(batch)
    consts = (meta, win, cw, cb, wa, wi, ba, bi, lam, wout, lng, lnb, wr, br)
    x_spec = pl.BlockSpec(
        (pl.Element(1), pl.Element(TS), pl.Element(D)),
        lambda b, j: (b, pl.multiple_of(jnp.maximum(j * TS - N_META_TOK, 0), SUBLANES), 0))
    return pl.pallas_call(
        _rglru_kernel,
        grid=(batch, nj),
        in_specs=[x_spec] + [_const_spec(c.shape) for c in consts],
        out_specs=out_specs,
        out_shape=out_shape,
        scratch_shapes=[
            pltpu.VMEM((TS, D), F32),
            pltpu.VMEM((2 * NCHUNK, TS, LANES), F32),
            pltpu.VMEM((TS, D), F32),
            pltpu.VMEM((TS, D), BF16),
            pltpu.VMEM((TS, 2 * D), F32),
            pltpu.VMEM((NCHUNK, TS, LANES), F32),
            pltpu.VMEM((3 * SUBLANES, D), F32),
            pltpu.VMEM((SUBLANES, D), F32),
            pltpu.VMEM((TS, TS), BF16),
            pltpu.VMEM((1, LANES), F32),
        ],
        compiler_params=pltpu.CompilerParams(
            dimension_semantics=("arbitrary", "arbitrary"), vmem_limit_bytes=VMEM_LIMIT),
        name="rglru_layer",
    )(x, *consts)


def _shortconv_layer(h, win, cw, wout, lng, lnb, wr, br):
    batch = h.shape[0]
    nj, out_shape, out_specs = _mixer_out(batch)
    consts = (win, cw, wout, lng, lnb, wr, br)
    return pl.pallas_call(
        _shortconv_kernel,
        grid=(batch, nj),
        in_specs=[pl.BlockSpec((1, TS, D), lambda b, j: (b, j, 0))] + [_const_spec(c.shape) for c in consts],
        out_specs=out_specs,
        out_shape=out_shape,
        scratch_shapes=[
            pltpu.VMEM((TS, 3 * D), F32),
            pltpu.VMEM((TS, D), F32),
            pltpu.VMEM((TS, D), BF16),
            pltpu.VMEM((SUBLANES, D), F32),
            pltpu.VMEM((TS, TS), BF16),
            pltpu.VMEM((1, LANES), F32),
        ],
        compiler_params=pltpu.CompilerParams(
            dimension_semantics=("arbitrary", "arbitrary"), vmem_limit_bytes=VMEM_LIMIT),
        name="shortconv_layer",
    )(h, *consts)


def _tile_rows(t):
    return pl.ds(pl.multiple_of(t * SUBLANES, SUBLANES), SUBLANES)


def _dispatch_kernel(pend_ref, padded_ref, nused_ref, dest_ref, h_ref, xs_ref, zero_scr, sem_z, sem):
    i = pl.program_id(0)
    n_blocks = xs_ref.shape[0] // (BM * SUBLANES)

    def zero_block(blk):
        start = pl.multiple_of(blk * (BM * SUBLANES), BM * SUBLANES)
        return pltpu.make_async_copy(zero_scr, xs_ref.at[pl.ds(start, BM * SUBLANES), :], sem_z)

    @pl.when(i == 0)
    def _():
        zero_scr[...] = jnp.zeros_like(zero_scr)
        for e in range(N_EXPERTS):
            @pl.when(padded_ref[e] > 0)
            def _():
                zero_block(pend_ref[e] // BM - 1).start()

        def fill_tail(blk, c):
            zero_block(blk).start()
            return c

        lax.fori_loop(nused_ref[0], n_blocks, fill_tail, 0)
        for e in range(N_EXPERTS):
            @pl.when(padded_ref[e] > 0)
            def _():
                zero_block(0).wait()

        def drain_tail(blk, c):
            zero_block(0).wait()
            return c

        lax.fori_loop(nused_ref[0], n_blocks, drain_tail, 0)

    def row_copy(t, d):
        return pltpu.make_async_copy(h_ref.at[_tile_rows(t), :], xs_ref.at[_tile_rows(d), :], sem)

    for jj in range(TTM // TS):
        def issue(i8, c, jj=jj):
            for u in range(ISSUE_UNROLL):
                tl = i8 * ISSUE_UNROLL + u
                for k in range(2):
                    row_copy(jj * TS + tl, dest_ref[0, 0, (2 * jj + k) * TS + tl]).start(priority=k)
            return c

        lax.fori_loop(0, TS // ISSUE_UNROLL, issue, 0)

    def drain(i8, c):
        for _ in range(2 * ISSUE_UNROLL):
            row_copy(0, 0).wait()
        return c

    lax.fori_loop(0, TTM // ISSUE_UNROLL, drain, 0)


def _dispatch(h1t, dest_m, pend, padded, n_used, n_rows):
    nt = dest_m.shape[0]
    return pl.pallas_call(
        _dispatch_kernel,
        grid_spec=pltpu.PrefetchScalarGridSpec(
            num_scalar_prefetch=3,
            grid=(nt,),
            in_specs=[pl.BlockSpec((1, 1, 2 * TTM), lambda i, *_: (i, 0, 0), memory_space=pltpu.SMEM),
                      pl.BlockSpec((TTM * SUBLANES, LANES), lambda i, *_: (i, 0))],
            out_specs=pl.BlockSpec(memory_space=pl.ANY),
            scratch_shapes=[pltpu.VMEM((BM * SUBLANES, LANES), F32),
                            pltpu.SemaphoreType.DMA(()),
                            pltpu.SemaphoreType.DMA(())],
        ),
        out_shape=jax.ShapeDtypeStruct((n_rows * SUBLANES, LANES), F32),
        compiler_params=pltpu.CompilerParams(dimension_semantics=("arbitrary",)),
        name="moe_dispatch",
    )(pend, padded, n_used, dest_m, h1t)


def _expert_kernel(be_ref, nu_ref, xs_ref, wg_ref, wu_ref, wd_ref, ys_ref, wg_b, wu_b, wd_b):
    i = pl.program_id(0)

    @pl.when(i < nu_ref[0])
    def _():
        @pl.when((i == 0) | (be_ref[i] != be_ref[jnp.maximum(i - 1, 0)]))
        def _():
            wg_b[...] = wg_ref[0, 0].astype(BF16)
            wu_b[...] = wu_ref[0, 0].astype(BF16)
            wd_b[...] = wd_ref[0, 0].astype(BF16)

        xb = _load_token_tiles(xs_ref, 0, BM).astype(BF16)
        g = jnp.dot(xb, wg_b[...], preferred_element_type=F32)
        u = jnp.dot(xb, wu_b[...], preferred_element_type=F32)
        hmid = (jax.nn.silu(g) * u).astype(BF16)
        _store_token_tiles(ys_ref, 0, jnp.dot(hmid, wd_b[...], preferred_element_type=F32))

    @pl.when(i >= nu_ref[0])
    def _():
        ys_ref[...] = jnp.zeros_like(ys_ref)


def _experts(xs, block_expert, n_used, wg, wu, wd, layer):
    blk_rows = BM * SUBLANES
    nb = xs.shape[0] // blk_rows
    return pl.pallas_call(
        _expert_kernel,
        grid_spec=pltpu.PrefetchScalarGridSpec(
            num_scalar_prefetch=2,
            grid=(nb,),
            in_specs=[pl.BlockSpec((blk_rows, LANES), lambda i, be, nu: (jnp.minimum(i, nu[0] - 1), 0)),
                      pl.BlockSpec((1, 1, D, D_EXPERT), lambda i, be, nu: (layer, be[i], 0, 0)),
                      pl.BlockSpec((1, 1, D, D_EXPERT), lambda i, be, nu: (layer, be[i], 0, 0)),
                      pl.BlockSpec((1, 1, D_EXPERT, D), lambda i, be, nu: (layer, be[i], 0, 0))],
            out_specs=pl.BlockSpec((blk_rows, LANES), lambda i, be, nu: (i, 0)),
            scratch_shapes=[pltpu.VMEM((D, D_EXPERT), BF16),
                            pltpu.VMEM((D, D_EXPERT), BF16),
                            pltpu.VMEM((D_EXPERT, D), BF16)],
        ),
        out_shape=jax.ShapeDtypeStruct(xs.shape, F32),
        compiler_params=pltpu.CompilerParams(
            dimension_semantics=("arbitrary",), vmem_limit_bytes=VMEM_LIMIT),
        name="moe_experts",
    )(block_expert, n_used, xs, wg, wu, wd, )


def _combine_kernel(dest_ref, h_ref, route_ref, lng_ref, lnb_ref, ys_ref, out_ref, y0_scr, y1_scr, sems,
                    *, half, lc, stride_half, stride_k):
    out2d = out_ref if len(out_ref.shape) == 2 else out_ref.at[0]

    def row_copy(d, buf, t, sem):
        return pltpu.make_async_copy(ys_ref.at[_tile_rows(d), :], buf.at[_tile_rows(t), :], sem)

    for hf in range(2):
        def issue(i8, c, hf=hf):
            for u in range(ISSUE_UNROLL):
                tl = i8 * ISSUE_UNROLL + u
                base = hf * stride_half + tl
                row_copy(dest_ref[0, 0, base], y0_scr, hf * half + tl, sems.at[hf]).start(priority=0)
                row_copy(dest_ref[0, 0, base + stride_k], y1_scr, hf * half + tl, sems.at[hf]).start(priority=1)
            return c

        lax.fori_loop(0, half // ISSUE_UNROLL, issue, 0)

    g = lng_ref[...]
    bta = lnb_ref[...]

    for hf in range(2):
        def drain(i8, c, hf=hf):
            for _ in range(2 * ISSUE_UNROLL):
                row_copy(0, y0_scr, 0, sems.at[hf]).wait()
            return c

        lax.fori_loop(0, half // ISSUE_UNROLL, drain, 0)

        for r in range(hf * half, (hf + 1) * half, lc):
            g1 = route_ref[r:r + lc, 4:5]
            g2 = route_ref[r:r + lc, 5:6]
            y = g1 * _load_token_tiles(y0_scr, r, lc) + g2 * _load_token_tiles(y1_scr, r, lc)
            z = ALPHA * _load_token_tiles(h_ref, r, lc) + y
            out2d[r:r + lc, :] = _layer_norm_rows(z, g, bta)


def _combine_scratch(tt):
    return [pltpu.VMEM((tt * SUBLANES, LANES), F32),
            pltpu.VMEM((tt * SUBLANES, LANES), F32),
            pltpu.SemaphoreType.DMA((2,))]


def _combine(h1t, route, dest_m, ys, lng, lnb):
    nt = dest_m.shape[0]
    body = functools.partial(_combine_kernel, half=TS, lc=LC, stride_half=2 * TS, stride_k=TS)
    return pl.pallas_call(
        body,
        grid=(nt,),
        in_specs=[pl.BlockSpec((1, 1, 2 * TTM), lambda i: (i, 0, 0), memory_space=pltpu.SMEM),
                  pl.BlockSpec((TTM * SUBLANES, LANES), lambda i: (i, 0)),
                  pl.BlockSpec((TTM, LANES), lambda i: (i, 0)),
                  pl.BlockSpec((1, D), lambda i: (0, 0)),
                  pl.BlockSpec((1, D), lambda i: (0, 0)),
                  pl.BlockSpec(memory_space=pl.ANY)],
        out_specs=pl.BlockSpec((TTM, D), lambda i: (i, 0)),
        out_shape=jax.ShapeDtypeStruct((nt * TTM, D), F32),
        scratch_shapes=_combine_scratch(TTM),
        compiler_params=pltpu.CompilerParams(dimension_semantics=("arbitrary",)),
        name="moe_combine",
    )(dest_m, h1t, route, lng, lnb, ys)


def _combine_final(h1t, route, dest_f, ys, lng, lnb, batch):
    per_seq = SEQ // TF

    def tok0(i):
        return pl.multiple_of((i // per_seq) * SEQ_TOT + N_META_TOK + (i % per_seq) * TF, SUBLANES)

    body = functools.partial(_combine_kernel, half=TF // 2, lc=LCF, stride_half=TF // 2, stride_k=TF)
    return pl.pallas_call(
        body,
        grid=(batch * per_seq,),
        in_specs=[pl.BlockSpec((1, 1, 2 * TF), lambda i: (i, 0, 0), memory_space=pltpu.SMEM),
                  pl.BlockSpec((pl.Element(TF * SUBLANES), pl.Element(LANES)),
                               lambda i: (pl.multiple_of(tok0(i) * SUBLANES, SUBLANES * SUBLANES), 0)),
                  pl.BlockSpec((pl.Element(TF), pl.Element(LANES)), lambda i: (tok0(i), 0)),
                  pl.BlockSpec((1, D), lambda i: (0, 0)),
                  pl.BlockSpec((1, D), lambda i: (0, 0)),
                  pl.BlockSpec(memory_space=pl.ANY)],
        out_specs=pl.BlockSpec((1, TF, D), lambda i: (i // per_seq, i % per_seq, 0)),
        out_shape=jax.ShapeDtypeStruct((batch, SEQ, D), F32),
        scratch_shapes=_combine_scratch(TF),
        compiler_params=pltpu.CompilerParams(dimension_semantics=("arbitrary",)),
        name="moe_combine_final",
    )(dest_f, h1t, route, lng, lnb, ys)


def _moe(h1t, route, rt, cnt, wg, wu, wd, layer, lng, lnb, final):
    batch, nj = rt.shape[0], rt.shape[1]
    t_tot = route.shape[0]
    n_blocks = (2 * t_tot + N_EXPERTS * (BM - 1) + BM - 1) // BM
    eid = rt[:, :, 0:2, :].astype(jnp.int32)
    rank = rt[:, :, 2:4, :].astype(jnp.int32)
    counts = cnt[0, :N_EXPERTS].astype(jnp.int32)
    padded = (counts + BM - 1) // BM * BM
    pend = jnp.cumsum(padded)
    pstart = pend - padded
    dest = pstart[eid] + rank
    dest_m = dest.reshape(batch * nj * TS // TTM, 1, 2 * TTM)
    n_used = pend[-1:] // BM
    blk = jnp.minimum(jnp.arange(n_blocks, dtype=jnp.int32), n_used[0] - 1) * BM
    block_expert = jnp.minimum(jnp.sum((pend[None, :] <= blk[:, None]).astype(jnp.int32), axis=1),
                               N_EXPERTS - 1)

    xs = _dispatch(h1t, dest_m, pend, padded, n_used, n_blocks * BM)
    ys = _experts(xs, block_expert, n_used, wg, wu, wd, layer)
    if not final:
        return _combine(h1t, route, dest_m, ys, lng, lnb)
    dest_f = dest.transpose(0, 2, 1, 3).reshape(batch, 2, SEQ_TOT)[:, :, N_META_TOK:]
    dest_f = dest_f.reshape(batch, 2, SEQ // TF, TF).transpose(0, 2, 1, 3).reshape(batch * (SEQ // TF), 1, 2 * TF)
    return _combine_final(h1t, route, dest_f, ys, lng, lnb, batch)


def _block_diag(w):
    per = GATE_BLK // HEAD_DIM
    w4 = w.reshape(D // GATE_BLK, per, HEAD_DIM, HEAD_DIM)
    out = jnp.zeros((D // GATE_BLK, GATE_BLK, GATE_BLK), w.dtype)
    for p in range(per):
        out = out.at[:, p * HEAD_DIM:(p + 1) * HEAD_DIM, p * HEAD_DIM:(p + 1) * HEAD_DIM].set(w4[:, p])
    return out


def _router_params(w_group, b_group, w_expert, b_expert):
    pad = LANES - N_GROUPS - N_EXPERTS
    wr = jnp.concatenate([w_group, w_expert, jnp.zeros((D, pad), F32)], axis=1).astype(BF16)
    br = jnp.concatenate([b_group, b_expert.reshape(-1), jnp.zeros((pad,), F32)])[None, :]
    return wr, br


def kernel(x, meta_tokens, lru_w_in, lru_conv_w, lru_conv_b, lru_w_a, lru_b_a, lru_w_i, lru_b_i, lru_lambda, lru_w_out, sc_w_in, sc_conv_w, sc_w_out, moe_w_group, moe_b_group, moe_w_expert, moe_b_expert, moe_w_gate, moe_w_up, moe_w_down, ln_g, ln_b):
    batch = x.shape[0]
    row = lambda v: v.reshape(1, -1)

    wr, br = _router_params(moe_w_group[0], moe_b_group[0], moe_w_expert[0], moe_b_expert[0])
    h1, route, rt, cnt = _rglru_layer(
        x, meta_tokens.astype(x.dtype), lru_w_in[0].astype(BF16), lru_conv_w[0], row(lru_conv_b[0]),
        _block_diag(lru_w_a[0]).astype(BF16), _block_diag(lru_w_i[0]).astype(BF16),
        row(lru_b_a[0]), row(lru_b_i[0]), row(lru_lambda[0]), lru_w_out[0].astype(BF16),
        row(ln_g[0, 0]), row(ln_b[0, 0]), wr, br)
    h2 = _moe(h1.reshape(-1, LANES), route.reshape(-1, LANES), rt, cnt,
              moe_w_gate, moe_w_up, moe_w_down, 0, row(ln_g[0, 1]), row(ln_b[0, 1]), final=False)

    wr, br = _router_params(moe_w_group[1], moe_b_group[1], moe_w_expert[1], moe_b_expert[1])
    h3, route, rt, cnt = _shortconv_layer(
        h2.reshape(batch, SEQ_TOT, D), sc_w_in[0].astype(BF16), sc_conv_w[0], sc_w_out[0].astype(BF16),
        row(ln_g[1, 0]), row(ln_b[1, 0]), wr, br)
    return _moe(h3.reshape(-1, LANES), route.reshape(-1, LANES), rt, cnt,
                moe_w_gate, moe_w_up, moe_w_down, 1, row(ln_g[1, 1]), row(ln_b[1, 1]), final=True)
```

```python
import functools

import jax
import jax.numpy as jnp
from jax import lax
from jax.experimental import pallas as pl
from jax.experimental.pallas import tpu as pltpu

F32 = jnp.float32
BF16 = jnp.bfloat16

D = 1024
SEQ = 8192
N_META_TOK = 16
SEQ_TOT = SEQ + N_META_TOK
LRU_HEADS = 16
HEAD_DIM = D // LRU_HEADS
LRU_C = 8.0
N_GROUPS = 4
EPG = 8
N_EXPERTS = N_GROUPS * EPG
D_EXPERT = D // 2
DEPTH = 2
ALPHA = (2.0 * DEPTH) ** 0.25
LN_EPS = 1e-5

LANES = 128
SUBLANES = 8
GATE_BLK = 256

TS = 432
SEG = TS // SUBLANES
TTM = 2 * TS
TF = 512
BM = 512
CH = 16
LC = 144
LCF = 128
WALK_UNROLL = 6
ISSUE_UNROLL = 8
NCHUNK = D // LANES

VMEM_LIMIT = 56 * 1024 * 1024


def _softplus(x):
    return jnp.maximum(x, 0.0) + jnp.log1p(jnp.exp(-jnp.abs(x)))


def _sigmoid(x):
    return 0.5 * jnp.tanh(0.5 * x) + 0.5


def _load_token_tiles(ref, r, n):
    return jnp.concatenate(
        [ref[pl.ds(r * SUBLANES + c, n, stride=SUBLANES), :] for c in range(NCHUNK)], axis=1)


def _store_token_tiles(ref, r, val):
    n = val.shape[0]
    for c in range(NCHUNK):
        ref[pl.ds(r * SUBLANES + c, n, stride=SUBLANES), :] = val[:, c * LANES:(c + 1) * LANES]


def _layer_norm_rows(z, g, bta):
    mu = jnp.mean(z, axis=-1, keepdims=True)
    zc = z - mu
    var = jnp.mean(zc * zc, axis=-1, keepdims=True)
    return zc * lax.rsqrt(var + LN_EPS) * g + bta


def _shifted(cur, prv, s, sub):
    return jnp.where(sub >= s, pltpu.roll(cur, s, 0), pltpu.roll(prv, s, 0))


def _exact_transpose_8(slab):
    sel = jnp.where(lax.broadcasted_iota(jnp.int32, (SUBLANES, LANES), 0)
                    == lax.broadcasted_iota(jnp.int32, (SUBLANES, LANES), 1), 1.0, 0.0).astype(BF16)
    p1 = slab.astype(BF16)
    r1 = slab - p1.astype(F32)
    p2 = r1.astype(BF16)
    p3 = (r1 - p2.astype(F32)).astype(BF16)
    dn = (((1,), (1,)), ((), ()))
    out = lax.dot_general(sel, p1, dn, preferred_element_type=F32)
    out = out + lax.dot_general(sel, p2, dn, preferred_element_type=F32)
    return out + lax.dot_general(sel, p3, dn, preferred_element_type=F32)


def _route_init(tri_scr, cnt_scr, logits_scr):
    ts = tri_scr.shape[0]
    row = lax.broadcasted_iota(jnp.int32, (ts, ts), 0)
    col = lax.broadcasted_iota(jnp.int32, (ts, ts), 1)
    tri_scr[...] = jnp.where(col < row, 1.0, 0.0).astype(BF16)
    cnt_scr[...] = jnp.zeros_like(cnt_scr)
    logits_scr[...] = jnp.zeros_like(logits_scr)


def _ln_logits(h2d, m_scr, lng_ref, lnb_ref, wr_ref, br_ref, h1_tiles, h1b_scr, logits_scr):
    ts = m_scr.shape[0]
    g = lng_ref[...]
    bta = lnb_ref[...]

    def ln_body(i, c):
        r = pl.multiple_of(i * LC, LC)
        z = ALPHA * h2d[pl.ds(r, LC), :] + m_scr[pl.ds(r, LC), :]
        y = _layer_norm_rows(z, g, bta)
        _store_token_tiles(h1_tiles, r, y)
        h1b_scr[pl.ds(r, LC), :] = y.astype(BF16)
        return c

    lax.fori_loop(0, ts // LC, ln_body, 0)

    logits_scr[...] = jnp.dot(h1b_scr[...], wr_ref[...], preferred_element_type=F32) + br_ref[...]


def _route(logits_scr, tri_scr, cnt_scr, route_ref, rt_ref, cnt_ref, valid):
    ts = logits_scr.shape[0]
    logits = logits_scr[...]
    lane = lax.broadcasted_iota(jnp.int32, (ts, LANES), 1).astype(F32)
    neg = jnp.float32(-jnp.inf)
    big = jnp.float32(LANES)

    gmask = lane < N_GROUPS
    gl = jnp.where(gmask, logits, neg)
    gmax = jnp.max(gl, axis=-1, keepdims=True)
    gidx = jnp.min(jnp.where(gl == gmax, lane, big), axis=-1, keepdims=True)
    sumexp = jnp.sum(jnp.where(gmask, jnp.exp(gl - gmax), 0.0), axis=-1, keepdims=True)
    pg = 1.0 / sumexp

    lo = N_GROUPS + EPG * gidx
    emask = (lane >= lo) & (lane < lo + EPG)
    el = jnp.where(emask, logits, neg)
    v1 = jnp.max(el, axis=-1, keepdims=True)
    i1 = jnp.min(jnp.where(el == v1, lane, big), axis=-1, keepdims=True)
    el2 = jnp.where(lane == i1, neg, el)
    v2 = jnp.max(el2, axis=-1, keepdims=True)
    i2 = jnp.min(jnp.where(el2 == v2, lane, big), axis=-1, keepdims=True)
    ex = jnp.exp(v2 - v1)
    den = 1.0 / (1.0 + ex)
    g1 = pg * den
    g2 = pg * ex * den
    e1 = i1 - N_GROUPS
    e2 = i2 - N_GROUPS

    sel1 = lane == e1
    sel2 = lane == e2
    onehot = jnp.where(sel1 | sel2, 1.0, 0.0)
    before = jnp.dot(tri_scr[...], onehot.astype(BF16), preferred_element_type=F32) + cnt_scr[...]
    rank1 = jnp.sum(jnp.where(sel1, before, 0.0), axis=-1, keepdims=True)
    rank2 = jnp.sum(jnp.where(sel2, before, 0.0), axis=-1, keepdims=True)
    cnt_scr[...] = cnt_scr[...] + valid * jnp.sum(onehot, axis=0, keepdims=True)

    slab = jnp.where(lane == 0, e1, 0.0)
    slab = jnp.where(lane == 1, e2, slab)
    slab = jnp.where(lane == 2, rank1, slab)
    slab = jnp.where(lane == 3, rank2, slab)
    slab = jnp.where(lane == 4, g1, slab)
    slab = jnp.where(lane == 5, g2, slab)
    route_ref[...] = slab
    rt_ref[0] = _exact_transpose_8(slab)
    cnt_ref[...] = cnt_scr[...]


def _seg_load(ref3, c0, g):
    return jnp.concatenate(
        [ref3[c0 + c, pl.ds(g, SUBLANES, stride=SEG), :] for c in range(NCHUNK)], axis=1)


def _seg_store(ref3, g, val):
    for c in range(NCHUNK):
        ref3[c, pl.ds(g, SUBLANES, stride=SEG), :] = val[:, c * LANES:(c + 1) * LANES]


def _rglru_kernel(x_ref, meta_ref, win_ref, cw_ref, cb_ref, wa_ref, wi_ref, ba_ref, bi_ref, lam_ref,
                  wout_ref, lng_ref, lnb_ref, wr_ref, br_ref,
                  h1_ref, route_ref, rt_ref, cnt_ref,
                  h_scr, u_scr, xc_scr, xcb_scr, g_scr, y_scr, halo_scr, carry_scr, tri_scr, cnt_scr,
                  logits_scr):
    s = pl.program_id(0)
    j = s % (SEQ_TOT // TS)

    @pl.when(s == 0)
    def _():
        _route_init(tri_scr, cnt_scr, logits_scr)

    @pl.when(j == 0)
    def _():
        halo_scr[...] = jnp.zeros_like(halo_scr)
        carry_scr[...] = jnp.zeros_like(carry_scr)
        h_scr[0:N_META_TOK, :] = meta_ref[...]
        h_scr[N_META_TOK:TS, :] = x_ref[0, 0:TS - N_META_TOK, :]

    @pl.when(j > 0)
    def _():
        h_scr[...] = x_ref[0]

    u = jnp.dot(h_scr[...].astype(BF16), win_ref[...], preferred_element_type=F32)
    for c in range(2 * NCHUNK):
        u_scr[c] = u[:, c * LANES:(c + 1) * LANES]

    _route(logits_scr, tri_scr, cnt_scr, route_ref, rt_ref, cnt_ref, jnp.where(s > 0, 1.0, 0.0))

    sub = lax.broadcasted_iota(jnp.int32, (SUBLANES, D), 0)
    cw = [cw_ref[k:k + 1, :] for k in range(4)]
    cb = cb_ref[...]

    tails = []
    for m in range(3):
        um = _seg_load(u_scr, NCHUNK, SEG - 3 + m)
        tails.append(jnp.where(sub >= 1, pltpu.roll(um, 1, 0),
                               pltpu.roll(halo_scr[m * SUBLANES:(m + 1) * SUBLANES, :], 1, 0)))
        halo_scr[m * SUBLANES:(m + 1) * SUBLANES, :] = um

    def conv_body(i, carry):
        x1, x2, x3 = carry
        outs = []
        for hh in range(CH // SUBLANES):
            cur = _seg_load(u_scr, NCHUNK, 2 * i + hh)
            outs.append(cur * cw[3] + x1 * cw[2] + x2 * cw[1] + x3 * cw[0] + cb)
            x1, x2, x3 = cur, x1, x2
        xc = jnp.concatenate(outs, axis=0)
        r = pl.multiple_of(i * CH, CH)
        xc_scr[pl.ds(r, CH), :] = xc
        xcb_scr[pl.ds(r, CH), :] = xc.astype(BF16)
        return x1, x2, x3

    lax.fori_loop(0, TS // CH, conv_body, (tails[2], tails[1], tails[0]))

    for k in range(D // GATE_BLK):
        xk = xcb_scr[:, k * GATE_BLK:(k + 1) * GATE_BLK]
        g_scr[:, k * GATE_BLK:(k + 1) * GATE_BLK] = jnp.dot(xk, wa_ref[k], preferred_element_type=F32)
        g_scr[:, D + k * GATE_BLK:D + (k + 1) * GATE_BLK] = jnp.dot(xk, wi_ref[k], preferred_element_type=F32)

    ba = ba_ref[...]
    bi = bi_ref[...]
    nsp = -LRU_C * _softplus(-lam_ref[...])

    def scan_body(i, carry):
        hl, pc = carry
        for hh in range(WALK_UNROLL):
            rows = pl.ds(pl.multiple_of((i * WALK_UNROLL + hh) * SUBLANES, SUBLANES), SUBLANES)
            xc = xc_scr[rows, :]
            ra = _sigmoid(g_scr[rows, :D] + ba)
            gi = _sigmoid(g_scr[rows, D:] + bi)
            log_a = ra * nsp
            a = jnp.exp(log_a)
            m2 = -jnp.tanh(log_a) * (a * a + 1.0)
            hl = a * hl + jnp.sqrt(m2) * (gi * xc)
            pc = a * pc
            g_scr[rows, :D] = hl
            g_scr[rows, D:] = pc
        return hl, pc

    hl_end, pc_end = lax.fori_loop(0, SEG // WALK_UNROLL, scan_body,
                                   (jnp.zeros((SUBLANES, D), F32), jnp.ones((SUBLANES, D), F32)))

    c0 = carry_scr[...]
    av, bv = pc_end, hl_end
    for s in (1, 2, 4):
        m = sub >= s
        a_s = pltpu.roll(av, s, 0)
        b_s = pltpu.roll(bv, s, 0)
        bv = jnp.where(m, av * b_s + bv, bv)
        av = jnp.where(m, av * a_s, av)
    h_end = av * c0 + bv
    cin = jnp.where(sub >= 1, pltpu.roll(h_end, 1, 0), c0)
    carry_scr[...] = jnp.broadcast_to(h_end[SUBLANES - 1:SUBLANES, :], (SUBLANES, D))

    def fix_body(i, c):
        for hh in range(WALK_UNROLL):
            g = i * WALK_UNROLL + hh
            rows = pl.ds(pl.multiple_of(g * SUBLANES, SUBLANES), SUBLANES)
            hs = g_scr[rows, :D] + g_scr[rows, D:] * cin
            _seg_store(y_scr, g, hs * jax.nn.gelu(_seg_load(u_scr, 0, g)))
        return c

    lax.fori_loop(0, SEG // WALK_UNROLL, fix_body, 0)

    y = jnp.concatenate([y_scr[c] for c in range(NCHUNK)], axis=1).astype(BF16)
    xc_scr[...] = jnp.dot(y, wout_ref[...], preferred_element_type=F32)
    _ln_logits(h_scr, xc_scr, lng_ref, lnb_ref, wr_ref, br_ref, h1_ref, xcb_scr, logits_scr)


def _shortconv_kernel(h_ref, win_ref, cw_ref, wout_ref, lng_ref, lnb_ref, wr_ref, br_ref,
                      h1_ref, route_ref, rt_ref, cnt_ref,
                      u_scr, m_scr, y_scr, halo_scr, tri_scr, cnt_scr, logits_scr):
    s = pl.program_id(0)
    j = s % (SEQ_TOT // TS)
    ts = m_scr.shape[0]

    @pl.when(s == 0)
    def _():
        _route_init(tri_scr, cnt_scr, logits_scr)

    @pl.when(j == 0)
    def _():
        halo_scr[...] = jnp.zeros_like(halo_scr)

    u_scr[...] = jnp.dot(h_ref[...].astype(BF16), win_ref[...], preferred_element_type=F32)

    _route(logits_scr, tri_scr, cnt_scr, route_ref, rt_ref, cnt_ref, jnp.where(s > 0, 1.0, 0.0))

    sub = lax.broadcasted_iota(jnp.int32, (SUBLANES, D), 0)
    cw = [cw_ref[k:k + 1, :] for k in range(3)]

    def conv_body(i, prv):
        r = pl.multiple_of(i * CH, CH)
        ys = []
        for hh in range(CH // SUBLANES):
            r0 = r + SUBLANES * hh
            cur = u_scr[pl.ds(r0, SUBLANES), D:2 * D] * u_scr[pl.ds(r0, SUBLANES), 2 * D:]
            acc = cur * cw[2]
            for sh in (1, 2):
                acc = acc + _shifted(cur, prv, sh, sub) * cw[2 - sh]
            ys.append(u_scr[pl.ds(r0, SUBLANES), :D] * acc)
            prv = cur
        y_scr[pl.ds(r, CH), :] = jnp.concatenate(ys, axis=0).astype(BF16)
        return prv

    halo_scr[...] = lax.fori_loop(0, ts // CH, conv_body, halo_scr[...])

    m_scr[...] = jnp.dot(y_scr[...], wout_ref[...], preferred_element_type=F32)
    _ln_logits(h_ref, m_scr, lng_ref, lnb_ref, wr_ref, br_ref, h1_ref, y_scr, logits_scr)


def _const_spec(shape):
    n = len(shape)
    return pl.BlockSpec(shape, lambda s: (0,) * n)


def _mixer_out(n_tiles):
    out_shape = (jax.ShapeDtypeStruct(((n_tiles + 1) * TS * SUBLANES, LANES), F32),
                 jax.ShapeDtypeStruct((n_tiles * TS, LANES), F32),
                 jax.ShapeDtypeStruct((n_tiles, SUBLANES, TS), F32),
                 jax.ShapeDtypeStruct((1, LANES), F32))
    prev = lambda s: jnp.maximum(s - 1, 0)
    out_specs = (pl.BlockSpec((TS * SUBLANES, LANES), lambda s: (s, 0)),
                 pl.BlockSpec((TS, LANES), lambda s: (prev(s), 0)),
                 pl.BlockSpec((1, SUBLANES, TS), lambda s: (prev(s), 0, 0)),
                 pl.BlockSpec((1, LANES), lambda s: (0, 0)))
    return out_shape, out_specs


def _rglru_layer(x, meta, win, cw, cb, wa, wi, ba, bi, lam, wout, lng, lnb, wr, br):
    batch = x.shape[0]
    nj = SEQ_TOT // TS
    n_tiles = batch * nj
    out_shape, out_specs = _mixer_out(n_tiles)
    consts = (meta, win, cw, cb, wa, wi, ba, bi, lam, wout, lng, lnb, wr, br)

    def x_index(s):
        t = jnp.minimum(s, n_tiles - 1)
        return (t // nj, pl.multiple_of(jnp.maximum((t % nj) * TS - N_META_TOK, 0), SUBLANES), 0)

    x_spec = pl.BlockSpec((pl.Element(1), pl.Element(TS), pl.Element(D)), x_index)
    return pl.pallas_call(
        _rglru_kernel,
        grid=(n_tiles + 1,),
        in_specs=[x_spec] + [_const_spec(c.shape) for c in consts],
        out_specs=out_specs,
        out_shape=out_shape,
        scratch_shapes=[
            pltpu.VMEM((TS, D), F32),
            pltpu.VMEM((2 * NCHUNK, TS, LANES), F32),
            pltpu.VMEM((TS, D), F32),
            pltpu.VMEM((TS, D), BF16),
            pltpu.VMEM((TS, 2 * D), F32),
            pltpu.VMEM((NCHUNK, TS, LANES), F32),
            pltpu.VMEM((3 * SUBLANES, D), F32),
            pltpu.VMEM((SUBLANES, D), F32),
            pltpu.VMEM((TS, TS), BF16),
            pltpu.VMEM((1, LANES), F32),
            pltpu.VMEM((TS, LANES), F32),
        ],
        compiler_params=pltpu.CompilerParams(
            dimension_semantics=("arbitrary",), vmem_limit_bytes=VMEM_LIMIT),
        name="rglru_layer",
    )(x, *consts)


def _shortconv_layer(h, win, cw, wout, lng, lnb, wr, br):
    n_tiles = h.shape[0] // TS
    out_shape, out_specs = _mixer_out(n_tiles)
    consts = (win, cw, wout, lng, lnb, wr, br)
    return pl.pallas_call(
        _shortconv_kernel,
        grid=(n_tiles + 1,),
        in_specs=[pl.BlockSpec((TS, D), lambda s: (jnp.minimum(s, n_tiles - 1), 0))]
        + [_const_spec(c.shape) for c in consts],
        out_specs=out_specs,
        out_shape=out_shape,
        scratch_shapes=[
            pltpu.VMEM((TS, 3 * D), F32),
            pltpu.VMEM((TS, D), F32),
            pltpu.VMEM((TS, D), BF16),
            pltpu.VMEM((SUBLANES, D), F32),
            pltpu.VMEM((TS, TS), BF16),
            pltpu.VMEM((1, LANES), F32),
            pltpu.VMEM((TS, LANES), F32),
        ],
        compiler_params=pltpu.CompilerParams(
            dimension_semantics=("arbitrary",), vmem_limit_bytes=VMEM_LIMIT),
        name="shortconv_layer",
    )(h, *consts)


def _tile_rows(t):
    return pl.ds(pl.multiple_of(t * SUBLANES, SUBLANES), SUBLANES)


def _dispatch_kernel(pend_ref, padded_ref, nused_ref, dest_ref, h_ref, xs_ref, zero_scr, sem_z, sem):
    i = pl.program_id(0)
    n_blocks = xs_ref.shape[0] // (BM * SUBLANES)

    def zero_block(blk):
        start = pl.multiple_of(blk * (BM * SUBLANES), BM * SUBLANES)
        return pltpu.make_async_copy(zero_scr, xs_ref.at[pl.ds(start, BM * SUBLANES), :], sem_z)

    @pl.when(i == 0)
    def _():
        zero_scr[...] = jnp.zeros_like(zero_scr)
        for e in range(N_EXPERTS):
            @pl.when(padded_ref[e] > 0)
            def _():
                zero_block(pend_ref[e] // BM - 1).start()

        def fill_tail(blk, c):
            zero_block(blk).start()
            return c

        lax.fori_loop(nused_ref[0], n_blocks, fill_tail, 0)
        for e in range(N_EXPERTS):
            @pl.when(padded_ref[e] > 0)
            def _():
                zero_block(0).wait()

        def drain_tail(blk, c):
            zero_block(0).wait()
            return c

        lax.fori_loop(nused_ref[0], n_blocks, drain_tail, 0)

    def row_copy(t, d):
        return pltpu.make_async_copy(h_ref.at[_tile_rows(t), :], xs_ref.at[_tile_rows(d), :], sem)

    for jj in range(TTM // TS):
        def issue(i8, c, jj=jj):
            for u in range(ISSUE_UNROLL):
                tl = i8 * ISSUE_UNROLL + u
                for k in range(2):
                    row_copy(jj * TS + tl, dest_ref[0, 0, (2 * jj + k) * TS + tl]).start(priority=k)
            return c

        lax.fori_loop(0, TS // ISSUE_UNROLL, issue, 0)

    def drain(i8, c):
        for _ in range(2 * ISSUE_UNROLL):
            row_copy(0, 0).wait()
        return c

    lax.fori_loop(0, TTM // ISSUE_UNROLL, drain, 0)


def _dispatch(h1t, dest_m, pend, padded, n_used, n_rows):
    nt = dest_m.shape[0]
    return pl.pallas_call(
        _dispatch_kernel,
        grid_spec=pltpu.PrefetchScalarGridSpec(
            num_scalar_prefetch=3,
            grid=(nt,),
            in_specs=[pl.BlockSpec((1, 1, 2 * TTM), lambda i, *_: (i, 0, 0), memory_space=pltpu.SMEM),
                      pl.BlockSpec((TTM * SUBLANES, LANES), lambda i, *_: (i, 0))],
            out_specs=pl.BlockSpec(memory_space=pl.ANY),
            scratch_shapes=[pltpu.VMEM((BM * SUBLANES, LANES), F32),
                            pltpu.SemaphoreType.DMA(()),
                            pltpu.SemaphoreType.DMA(())],
        ),
        out_shape=jax.ShapeDtypeStruct((n_rows * SUBLANES, LANES), F32),
        compiler_params=pltpu.CompilerParams(dimension_semantics=("arbitrary",)),
        name="moe_dispatch",
    )(pend, padded, n_used, dest_m, h1t)


def _expert_kernel(be_ref, nu_ref, xs_ref, wg_ref, wu_ref, wd_ref, ys_ref, wg_b, wu_b, wd_b):
    i = pl.program_id(0)

    @pl.when(i < nu_ref[0])
    def _():
        @pl.when((i == 0) | (be_ref[i] != be_ref[jnp.maximum(i - 1, 0)]))
        def _():
            wg_b[...] = wg_ref[0, 0].astype(BF16)
            wu_b[...] = wu_ref[0, 0].astype(BF16)
            wd_b[...] = wd_ref[0, 0].astype(BF16)

        xb = _load_token_tiles(xs_ref, 0, BM).astype(BF16)
        g = jnp.dot(xb, wg_b[...], preferred_element_type=F32)
        u = jnp.dot(xb, wu_b[...], preferred_element_type=F32)
        hmid = (jax.nn.silu(g) * u).astype(BF16)
        _store_token_tiles(ys_ref, 0, jnp.dot(hmid, wd_b[...], preferred_element_type=F32))

    @pl.when(i >= nu_ref[0])
    def _():
        ys_ref[...] = jnp.zeros_like(ys_ref)


def _experts(xs, block_expert, n_used, wg, wu, wd, layer):
    blk_rows = BM * SUBLANES
    nb = xs.shape[0] // blk_rows
    return pl.pallas_call(
        _expert_kernel,
        grid_spec=pltpu.PrefetchScalarGridSpec(
            num_scalar_prefetch=2,
            grid=(nb,),
            in_specs=[pl.BlockSpec((blk_rows, LANES), lambda i, be, nu: (jnp.minimum(i, nu[0] - 1), 0)),
                      pl.BlockSpec((1, 1, D, D_EXPERT), lambda i, be, nu: (layer, be[i], 0, 0)),
                      pl.BlockSpec((1, 1, D, D_EXPERT), lambda i, be, nu: (layer, be[i], 0, 0)),
                      pl.BlockSpec((1, 1, D_EXPERT, D), lambda i, be, nu: (layer, be[i], 0, 0))],
            out_specs=pl.BlockSpec((blk_rows, LANES), lambda i, be, nu: (i, 0)),
            scratch_shapes=[pltpu.VMEM((D, D_EXPERT), BF16),
                            pltpu.VMEM((D, D_EXPERT), BF16),
                            pltpu.VMEM((D_EXPERT, D), BF16)],
        ),
        out_shape=jax.ShapeDtypeStruct(xs.shape, F32),
        compiler_params=pltpu.CompilerParams(
            dimension_semantics=("arbitrary",), vmem_limit_bytes=VMEM_LIMIT),
        name="moe_experts",
    )(block_expert, n_used, xs, wg, wu, wd, )


def _combine_kernel(dest_ref, h_ref, route_ref, lng_ref, lnb_ref, ys_ref, out_ref, y0_scr, y1_scr, sems,
                    *, half, lc, stride_half, stride_k):
    out2d = out_ref if len(out_ref.shape) == 2 else out_ref.at[0]

    def row_copy(d, buf, t, sem):
        return pltpu.make_async_copy(ys_ref.at[_tile_rows(d), :], buf.at[_tile_rows(t), :], sem)

    for hf in range(2):
        def issue(i8, c, hf=hf):
            for u in range(ISSUE_UNROLL):
                tl = i8 * ISSUE_UNROLL + u
                base = hf * stride_half + tl
                row_copy(dest_ref[0, 0, base], y0_scr, hf * half + tl, sems.at[hf]).start(priority=0)
                row_copy(dest_ref[0, 0, base + stride_k], y1_scr, hf * half + tl, sems.at[hf]).start(priority=1)
            return c

        lax.fori_loop(0, half // ISSUE_UNROLL, issue, 0)

    g = lng_ref[...]
    bta = lnb_ref[...]

    for hf in range(2):
        def drain(i8, c, hf=hf):
            for _ in range(2 * ISSUE_UNROLL):
                row_copy(0, y0_scr, 0, sems.at[hf]).wait()
            return c

        lax.fori_loop(0, half // ISSUE_UNROLL, drain, 0)

        for r in range(hf * half, (hf + 1) * half, lc):
            g1 = route_ref[r:r + lc, 4:5]
            g2 = route_ref[r:r + lc, 5:6]
            y = g1 * _load_token_tiles(y0_scr, r, lc) + g2 * _load_token_tiles(y1_scr, r, lc)
            z = ALPHA * _load_token_tiles(h_ref, r, lc) + y
            out2d[r:r + lc, :] = _layer_norm_rows(z, g, bta)


def _combine_scratch(tt):
    return [pltpu.VMEM((tt * SUBLANES, LANES), F32),
            pltpu.VMEM((tt * SUBLANES, LANES), F32),
            pltpu.SemaphoreType.DMA((2,))]


def _combine(h1t, route, dest_m, ys, lng, lnb):
    nt = dest_m.shape[0]
    body = functools.partial(_combine_kernel, half=TS, lc=LC, stride_half=2 * TS, stride_k=TS)
    return pl.pallas_call(
        body,
        grid=(nt,),
        in_specs=[pl.BlockSpec((1, 1, 2 * TTM), lambda i: (i, 0, 0), memory_space=pltpu.SMEM),
                  pl.BlockSpec((TTM * SUBLANES, LANES), lambda i: (i, 0)),
                  pl.BlockSpec((TTM, LANES), lambda i: (i, 0)),
                  pl.BlockSpec((1, D), lambda i: (0, 0)),
                  pl.BlockSpec((1, D), lambda i: (0, 0)),
                  pl.BlockSpec(memory_space=pl.ANY)],
        out_specs=pl.BlockSpec((TTM, D), lambda i: (i, 0)),
        out_shape=jax.ShapeDtypeStruct((nt * TTM, D), F32),
        scratch_shapes=_combine_scratch(TTM),
        compiler_params=pltpu.CompilerParams(dimension_semantics=("arbitrary",)),
        name="moe_combine",
    )(dest_m, h1t, route, lng, lnb, ys)


def _combine_final(h1t, route, dest_f, ys, lng, lnb, batch):
    per_seq = SEQ // TF

    def tok0(i):
        return pl.multiple_of((i // per_seq) * SEQ_TOT + N_META_TOK + (i % per_seq) * TF, SUBLANES)

    body = functools.partial(_combine_kernel, half=TF // 2, lc=LCF, stride_half=TF // 2, stride_k=TF)
    return pl.pallas_call(
        body,
        grid=(batch * per_seq,),
        in_specs=[pl.BlockSpec((1, 1, 2 * TF), lambda i: (i, 0, 0), memory_space=pltpu.SMEM),
                  pl.BlockSpec((pl.Element(TF * SUBLANES), pl.Element(LANES)),
                               lambda i: (pl.multiple_of(tok0(i) * SUBLANES, SUBLANES * SUBLANES), 0)),
                  pl.BlockSpec((pl.Element(TF), pl.Element(LANES)), lambda i: (tok0(i), 0)),
                  pl.BlockSpec((1, D), lambda i: (0, 0)),
                  pl.BlockSpec((1, D), lambda i: (0, 0)),
                  pl.BlockSpec(memory_space=pl.ANY)],
        out_specs=pl.BlockSpec((1, TF, D), lambda i: (i // per_seq, i % per_seq, 0)),
        out_shape=jax.ShapeDtypeStruct((batch, SEQ, D), F32),
        scratch_shapes=_combine_scratch(TF),
        compiler_params=pltpu.CompilerParams(dimension_semantics=("arbitrary",)),
        name="moe_combine_final",
    )(dest_f, h1t, route, lng, lnb, ys)


def _moe(h1t, route, rt, cnt, wg, wu, wd, layer, lng, lnb, batch, final):
    t_tot = route.shape[0]
    nj = t_tot // (batch * TS)
    n_blocks = (2 * t_tot + N_EXPERTS * (BM - 1) + BM - 1) // BM
    rt = rt.reshape(batch, nj, SUBLANES, TS)
    eid = rt[:, :, 0:2, :].astype(jnp.int32)
    rank = rt[:, :, 2:4, :].astype(jnp.int32)
    counts = cnt[0, :N_EXPERTS].astype(jnp.int32)
    padded = (counts + BM - 1) // BM * BM
    pend = jnp.cumsum(padded)
    dest = rank
    for e in range(1, N_EXPERTS):
        dest = dest + jnp.where(eid >= e, padded[e - 1], 0)
    dest_m = dest.reshape(batch * nj * TS // TTM, 1, 2 * TTM)
    n_used = pend[-1:] // BM
    blk = jnp.minimum(jnp.arange(n_blocks, dtype=jnp.int32), n_used[0] - 1) * BM
    block_expert = jnp.minimum(jnp.sum((pend[None, :] <= blk[:, None]).astype(jnp.int32), axis=1),
                               N_EXPERTS - 1)

    xs = _dispatch(h1t, dest_m, pend, padded, n_used, n_blocks * BM)
    ys = _experts(xs, block_expert, n_used, wg, wu, wd, layer)
    if not final:
        return _combine(h1t, route, dest_m, ys, lng, lnb)
    dest_f = dest.transpose(0, 2, 1, 3).reshape(batch, 2, SEQ_TOT)[:, :, N_META_TOK:]
    dest_f = dest_f.reshape(batch, 2, SEQ // TF, TF).transpose(0, 2, 1, 3).reshape(batch * (SEQ // TF), 1, 2 * TF)
    return _combine_final(h1t, route, dest_f, ys, lng, lnb, batch)


def _block_diag(w):
    per = GATE_BLK // HEAD_DIM
    w4 = w.reshape(D // GATE_BLK, per, HEAD_DIM, HEAD_DIM)
    out = jnp.zeros((D // GATE_BLK, GATE_BLK, GATE_BLK), w.dtype)
    for p in range(per):
        out = out.at[:, p * HEAD_DIM:(p + 1) * HEAD_DIM, p * HEAD_DIM:(p + 1) * HEAD_DIM].set(w4[:, p])
    return out


def _router_params(w_group, b_group, w_expert, b_expert):
    pad = LANES - N_GROUPS - N_EXPERTS
    wr = jnp.concatenate([w_group, w_expert, jnp.zeros((D, pad), F32)], axis=1).astype(BF16)
    br = jnp.concatenate([b_group, b_expert.reshape(-1), jnp.zeros((pad,), F32)])[None, :]
    return wr, br


def kernel(x, meta_tokens, lru_w_in, lru_conv_w, lru_conv_b, lru_w_a, lru_b_a, lru_w_i, lru_b_i, lru_lambda, lru_w_out, sc_w_in, sc_conv_w, sc_w_out, moe_w_group, moe_b_group, moe_w_expert, moe_b_expert, moe_w_gate, moe_w_up, moe_w_down, ln_g, ln_b):
    batch = x.shape[0]
    row = lambda v: v.reshape(1, -1)

    wr, br = _router_params(moe_w_group[0], moe_b_group[0], moe_w_expert[0], moe_b_expert[0])
    h1, route, rt, cnt = _rglru_layer(
        x, meta_tokens.astype(x.dtype), lru_w_in[0].astype(BF16), lru_conv_w[0], row(lru_conv_b[0]),
        _block_diag(lru_w_a[0]).astype(BF16), _block_diag(lru_w_i[0]).astype(BF16),
        row(lru_b_a[0]), row(lru_b_i[0]), row(lru_lambda[0]), lru_w_out[0].astype(BF16),
        row(ln_g[0, 0]), row(ln_b[0, 0]), wr, br)
    h2 = _moe(h1, route, rt, cnt, moe_w_gate, moe_w_up, moe_w_down, 0,
              row(ln_g[0, 1]), row(ln_b[0, 1]), batch, final=False)

    wr, br = _router_params(moe_w_group[1], moe_b_group[1], moe_w_expert[1], moe_b_expert[1])
    h3, route, rt, cnt = _shortconv_layer(
        h2, sc_w_in[0].astype(BF16), sc_conv_w[0], sc_w_out[0].astype(BF16),
        row(ln_g[1, 0]), row(ln_b[1, 0]), wr, br)
    return _moe(h3, route, rt, cnt, moe_w_gate, moe_w_up, moe_w_down, 1,
                row(ln_g[1, 1]), row(ln_b[1, 1]), batch, final=True)
```

```python
import functools

import jax
import jax.numpy as jnp
from jax import lax
from jax.experimental import pallas as pl
from jax.experimental.pallas import tpu as pltpu

F32 = jnp.float32
BF16 = jnp.bfloat16

D = 1024
SEQ = 8192
N_META_TOK = 16
SEQ_TOT = SEQ + N_META_TOK
LRU_HEADS = 16
HEAD_DIM = D // LRU_HEADS
LRU_C = 8.0
N_GROUPS = 4
EPG = 8
N_EXPERTS = N_GROUPS * EPG
D_EXPERT = D // 2
DEPTH = 2
ALPHA = (2.0 * DEPTH) ** 0.25
LN_EPS = 1e-5

LANES = 128
SUBLANES = 8
GATE_BLK = 256

TS = 432
SEG = TS // SUBLANES
TTM = 2 * TS
TF = 1024
BM = 512
CH = 16
LC = 144
LCC = 216
LCF = 256
WALK_UNROLL = 9
ISSUE_UNROLL = 16
NCHUNK = D // LANES

VMEM_LIMIT = 56 * 1024 * 1024


def _softplus(x):
    return jnp.maximum(x, 0.0) + jnp.log1p(jnp.exp(-jnp.abs(x)))


def _sigmoid(x):
    return 0.5 * jnp.tanh(0.5 * x) + 0.5


def _load_token_tiles(ref, r, n):
    return jnp.concatenate(
        [ref[pl.ds(r * SUBLANES + c, n, stride=SUBLANES), :] for c in range(NCHUNK)], axis=1)


def _store_token_tiles(ref, r, val):
    n = val.shape[0]
    for c in range(NCHUNK):
        ref[pl.ds(r * SUBLANES + c, n, stride=SUBLANES), :] = val[:, c * LANES:(c + 1) * LANES]


def _layer_norm_rows(z, g, bta):
    mu = jnp.mean(z, axis=-1, keepdims=True)
    zc = z - mu
    var = jnp.mean(zc * zc, axis=-1, keepdims=True)
    return zc * lax.rsqrt(var + LN_EPS) * g + bta


def _shifted(cur, prv, s, sub):
    return jnp.where(sub >= s, pltpu.roll(cur, s, 0), pltpu.roll(prv, s, 0))


def _exact_transpose_8(slab):
    sel = jnp.where(lax.broadcasted_iota(jnp.int32, (SUBLANES, LANES), 0)
                    == lax.broadcasted_iota(jnp.int32, (SUBLANES, LANES), 1), 1.0, 0.0).astype(BF16)
    p1 = slab.astype(BF16)
    r1 = slab - p1.astype(F32)
    p2 = r1.astype(BF16)
    p3 = (r1 - p2.astype(F32)).astype(BF16)
    dn = (((1,), (1,)), ((), ()))
    out = lax.dot_general(sel, p1, dn, preferred_element_type=F32)
    out = out + lax.dot_general(sel, p2, dn, preferred_element_type=F32)
    return out + lax.dot_general(sel, p3, dn, preferred_element_type=F32)


def _route_init(tri_scr, cnt_scr, logits_scr):
    ts = tri_scr.shape[0]
    row = lax.broadcasted_iota(jnp.int32, (ts, ts), 0)
    col = lax.broadcasted_iota(jnp.int32, (ts, ts), 1)
    tri_scr[...] = jnp.where(col < row, 1.0, 0.0).astype(BF16)
    cnt_scr[...] = jnp.zeros_like(cnt_scr)
    logits_scr[...] = jnp.zeros_like(logits_scr)


def _ln_logits(h2d, m_scr, lng_ref, lnb_ref, wr_ref, br_ref, h1_tiles, h1b_scr, logits_scr):
    ts = m_scr.shape[0]
    g = lng_ref[...]
    bta = lnb_ref[...]

    def ln_body(i, c):
        r = pl.multiple_of(i * LC, LC)
        z = ALPHA * h2d[pl.ds(r, LC), :] + m_scr[pl.ds(r, LC), :]
        y = _layer_norm_rows(z, g, bta)
        _store_token_tiles(h1_tiles, r, y)
        h1b_scr[pl.ds(r, LC), :] = y.astype(BF16)
        return c

    lax.fori_loop(0, ts // LC, ln_body, 0)

    logits_scr[...] = jnp.dot(h1b_scr[...], wr_ref[...], preferred_element_type=F32) + br_ref[...]


def _route(logits_scr, tri_scr, cnt_scr, route_ref, rt_ref, cnt_ref, valid):
    ts = logits_scr.shape[0]
    logits = logits_scr[...]
    lane = lax.broadcasted_iota(jnp.int32, (ts, LANES), 1).astype(F32)
    neg = jnp.float32(-jnp.inf)
    big = jnp.float32(LANES)

    gmask = lane < N_GROUPS
    gl = jnp.where(gmask, logits, neg)
    gmax = jnp.max(gl, axis=-1, keepdims=True)
    gidx = jnp.min(jnp.where(gl == gmax, lane, big), axis=-1, keepdims=True)
    sumexp = jnp.sum(jnp.where(gmask, jnp.exp(gl - gmax), 0.0), axis=-1, keepdims=True)
    pg = 1.0 / sumexp

    lo = N_GROUPS + EPG * gidx
    emask = (lane >= lo) & (lane < lo + EPG)
    el = jnp.where(emask, logits, neg)
    v1 = jnp.max(el, axis=-1, keepdims=True)
    i1 = jnp.min(jnp.where(el == v1, lane, big), axis=-1, keepdims=True)
    el2 = jnp.where(lane == i1, neg, el)
    v2 = jnp.max(el2, axis=-1, keepdims=True)
    i2 = jnp.min(jnp.where(el2 == v2, lane, big), axis=-1, keepdims=True)
    ex = jnp.exp(v2 - v1)
    den = 1.0 / (1.0 + ex)
    g1 = pg * den
    g2 = pg * ex * den
    e1 = i1 - N_GROUPS
    e2 = i2 - N_GROUPS

    sel1 = lane == e1
    sel2 = lane == e2
    onehot = jnp.where(sel1 | sel2, 1.0, 0.0)
    before = jnp.dot(tri_scr[...], onehot.astype(BF16), preferred_element_type=F32) + cnt_scr[...]
    rank1 = jnp.sum(jnp.where(sel1, before, 0.0), axis=-1, keepdims=True)
    rank2 = jnp.sum(jnp.where(sel2, before, 0.0), axis=-1, keepdims=True)
    cnt_scr[...] = cnt_scr[...] + valid * jnp.sum(onehot, axis=0, keepdims=True)

    slab = jnp.where(lane == 0, e1, 0.0)
    slab = jnp.where(lane == 1, e2, slab)
    slab = jnp.where(lane == 2, rank1, slab)
    slab = jnp.where(lane == 3, rank2, slab)
    slab = jnp.where(lane == 4, g1, slab)
    slab = jnp.where(lane == 5, g2, slab)
    route_ref[...] = slab
    rt_ref[0] = _exact_transpose_8(slab)
    cnt_ref[...] = cnt_scr[...]


def _seg_load(ref3, c0, g):
    return jnp.concatenate(
        [ref3[c0 + c, pl.ds(g, SUBLANES, stride=SEG), :] for c in range(NCHUNK)], axis=1)


def _seg_store(ref3, g, val):
    for c in range(NCHUNK):
        ref3[c, pl.ds(g, SUBLANES, stride=SEG), :] = val[:, c * LANES:(c + 1) * LANES]


def _rglru_kernel(x_ref, meta_ref, win_ref, cw_ref, cb_ref, wa_ref, wi_ref, ba_ref, bi_ref, lam_ref,
                  wout_ref, lng_ref, lnb_ref, wr_ref, br_ref,
                  h1_ref, route_ref, rt_ref, cnt_ref,
                  h_scr, u_scr, xc_scr, xcb_scr, g_scr, y_scr, halo_scr, carry_scr, tri_scr, cnt_scr,
                  logits_scr):
    s = pl.program_id(0)
    j = s % (SEQ_TOT // TS)

    @pl.when(s == 0)
    def _():
        _route_init(tri_scr, cnt_scr, logits_scr)

    @pl.when(j == 0)
    def _():
        halo_scr[...] = jnp.zeros_like(halo_scr)
        carry_scr[...] = jnp.zeros_like(carry_scr)
        h_scr[0:N_META_TOK, :] = meta_ref[...]
        h_scr[N_META_TOK:TS, :] = x_ref[0, 0:TS - N_META_TOK, :]

    @pl.when(j > 0)
    def _():
        h_scr[...] = x_ref[0]

    u = jnp.dot(h_scr[...].astype(BF16), win_ref[...], preferred_element_type=F32)
    for c in range(2 * NCHUNK):
        u_scr[c] = u[:, c * LANES:(c + 1) * LANES]

    _route(logits_scr, tri_scr, cnt_scr, route_ref, rt_ref, cnt_ref, jnp.where(s > 0, 1.0, 0.0))

    sub = lax.broadcasted_iota(jnp.int32, (SUBLANES, D), 0)
    cw = [cw_ref[k:k + 1, :] for k in range(4)]
    cb = cb_ref[...]

    tails = []
    for m in range(3):
        um = _seg_load(u_scr, NCHUNK, SEG - 3 + m)
        tails.append(jnp.where(sub >= 1, pltpu.roll(um, 1, 0),
                               pltpu.roll(halo_scr[m * SUBLANES:(m + 1) * SUBLANES, :], 1, 0)))
        halo_scr[m * SUBLANES:(m + 1) * SUBLANES, :] = um

    def conv_body(i, carry):
        x1, x2, x3 = carry
        outs = []
        for hh in range(CH // SUBLANES):
            cur = _seg_load(u_scr, NCHUNK, 2 * i + hh)
            outs.append(cur * cw[3] + x1 * cw[2] + x2 * cw[1] + x3 * cw[0] + cb)
            x1, x2, x3 = cur, x1, x2
        xc = jnp.concatenate(outs, axis=0)
        r = pl.multiple_of(i * CH, CH)
        xc_scr[pl.ds(r, CH), :] = xc
        xcb_scr[pl.ds(r, CH), :] = xc.astype(BF16)
        return x1, x2, x3

    lax.fori_loop(0, TS // CH, conv_body, (tails[2], tails[1], tails[0]))

    for k in range(D // GATE_BLK):
        xk = xcb_scr[:, k * GATE_BLK:(k + 1) * GATE_BLK]
        g_scr[:, k * GATE_BLK:(k + 1) * GATE_BLK] = jnp.dot(xk, wa_ref[k], preferred_element_type=F32)
        g_scr[:, D + k * GATE_BLK:D + (k + 1) * GATE_BLK] = jnp.dot(xk, wi_ref[k], preferred_element_type=F32)

    ba = ba_ref[...]
    bi = bi_ref[...]
    nsp = -LRU_C * _softplus(-lam_ref[...])

    def scan_body(i, carry):
        hl, pc = carry
        for hh in range(WALK_UNROLL):
            rows = pl.ds(pl.multiple_of((i * WALK_UNROLL + hh) * SUBLANES, SUBLANES), SUBLANES)
            xc = xc_scr[rows, :]
            ra = _sigmoid(g_scr[rows, :D] + ba)
            gi = _sigmoid(g_scr[rows, D:] + bi)
            log_a = ra * nsp
            a = jnp.exp(log_a)
            m2 = -jnp.tanh(log_a) * (a * a + 1.0)
            hl = a * hl + jnp.sqrt(m2) * (gi * xc)
            pc = a * pc
            g_scr[rows, :D] = hl
            g_scr[rows, D:] = pc
        return hl, pc

    hl_end, pc_end = lax.fori_loop(0, SEG // WALK_UNROLL, scan_body,
                                   (jnp.zeros((SUBLANES, D), F32), jnp.ones((SUBLANES, D), F32)))

    c0 = carry_scr[...]
    av, bv = pc_end, hl_end
    for sh in (1, 2, 4):
        m = sub >= sh
        a_s = pltpu.roll(av, sh, 0)
        b_s = pltpu.roll(bv, sh, 0)
        bv = jnp.where(m, av * b_s + bv, bv)
        av = jnp.where(m, av * a_s, av)
    h_end = av * c0 + bv
    cin = jnp.where(sub >= 1, pltpu.roll(h_end, 1, 0), c0)
    carry_scr[...] = jnp.broadcast_to(h_end[SUBLANES - 1:SUBLANES, :], (SUBLANES, D))

    def fix_body(i, c):
        for hh in range(WALK_UNROLL):
            g = i * WALK_UNROLL + hh
            rows = pl.ds(pl.multiple_of(g * SUBLANES, SUBLANES), SUBLANES)
            hs = g_scr[rows, :D] + g_scr[rows, D:] * cin
            _seg_store(y_scr, g, hs * jax.nn.gelu(_seg_load(u_scr, 0, g)))
        return c

    lax.fori_loop(0, SEG // WALK_UNROLL, fix_body, 0)

    y = jnp.concatenate([y_scr[c] for c in range(NCHUNK)], axis=1).astype(BF16)
    xc_scr[...] = jnp.dot(y, wout_ref[...], preferred_element_type=F32)
    _ln_logits(h_scr, xc_scr, lng_ref, lnb_ref, wr_ref, br_ref, h1_ref, xcb_scr, logits_scr)


def _shortconv_kernel(h_ref, win_ref, cw_ref, wout_ref, lng_ref, lnb_ref, wr_ref, br_ref,
                      h1_ref, route_ref, rt_ref, cnt_ref,
                      u_scr, m_scr, y_scr, halo_scr, tri_scr, cnt_scr, logits_scr):
    s = pl.program_id(0)
    j = s % (SEQ_TOT // TS)
    ts = m_scr.shape[0]

    @pl.when(s == 0)
    def _():
        _route_init(tri_scr, cnt_scr, logits_scr)

    @pl.when(j == 0)
    def _():
        halo_scr[...] = jnp.zeros_like(halo_scr)

    u_scr[...] = jnp.dot(h_ref[...].astype(BF16), win_ref[...], preferred_element_type=F32)

    _route(logits_scr, tri_scr, cnt_scr, route_ref, rt_ref, cnt_ref, jnp.where(s > 0, 1.0, 0.0))

    sub = lax.broadcasted_iota(jnp.int32, (SUBLANES, D), 0)
    cw = [cw_ref[k:k + 1, :] for k in range(3)]

    def conv_body(i, prv):
        r = pl.multiple_of(i * CH, CH)
        ys = []
        for hh in range(CH // SUBLANES):
            r0 = r + SUBLANES * hh
            cur = u_scr[pl.ds(r0, SUBLANES), D:2 * D] * u_scr[pl.ds(r0, SUBLANES), 2 * D:]
            acc = cur * cw[2]
            for sh in (1, 2):
                acc = acc + _shifted(cur, prv, sh, sub) * cw[2 - sh]
            ys.append(u_scr[pl.ds(r0, SUBLANES), :D] * acc)
            prv = cur
        y_scr[pl.ds(r, CH), :] = jnp.concatenate(ys, axis=0).astype(BF16)
        return prv

    halo_scr[...] = lax.fori_loop(0, ts // CH, conv_body, halo_scr[...])

    m_scr[...] = jnp.dot(y_scr[...], wout_ref[...], preferred_element_type=F32)
    _ln_logits(h_ref, m_scr, lng_ref, lnb_ref, wr_ref, br_ref, h1_ref, y_scr, logits_scr)


def _const_spec(shape):
    n = len(shape)
    return pl.BlockSpec(shape, lambda s: (0,) * n)


def _mixer_out(n_tiles):
    out_shape = (jax.ShapeDtypeStruct(((n_tiles + 1) * TS * SUBLANES, LANES), F32),
                 jax.ShapeDtypeStruct((n_tiles * TS, LANES), F32),
                 jax.ShapeDtypeStruct((n_tiles, SUBLANES, TS), F32),
                 jax.ShapeDtypeStruct((1, LANES), F32))
    prev = lambda s: jnp.maximum(s - 1, 0)
    out_specs = (pl.BlockSpec((TS * SUBLANES, LANES), lambda s: (s, 0)),
                 pl.BlockSpec((TS, LANES), lambda s: (prev(s), 0)),
                 pl.BlockSpec((1, SUBLANES, TS), lambda s: (prev(s), 0, 0)),
                 pl.BlockSpec((1, LANES), lambda s: (0, 0)))
    return out_shape, out_specs


def _rglru_layer(x, meta, win, cw, cb, wa, wi, ba, bi, lam, wout, lng, lnb, wr, br):
    batch = x.shape[0]
    nj = SEQ_TOT // TS
    n_tiles = batch * nj
    out_shape, out_specs = _mixer_out(n_tiles)
    consts = (meta, win, cw, cb, wa, wi, ba, bi, lam, wout, lng, lnb, wr, br)

    def x_index(s):
        t = jnp.minimum(s, n_tiles - 1)
        return (t // nj, pl.multiple_of(jnp.maximum((t % nj) * TS - N_META_TOK, 0), SUBLANES), 0)

    x_spec = pl.BlockSpec((pl.Element(1), pl.Element(TS), pl.Element(D)), x_index)
    return pl.pallas_call(
        _rglru_kernel,
        grid=(n_tiles + 1,),
        in_specs=[x_spec] + [_const_spec(c.shape) for c in consts],
        out_specs=out_specs,
        out_shape=out_shape,
        scratch_shapes=[
            pltpu.VMEM((TS, D), F32),
            pltpu.VMEM((2 * NCHUNK, TS, LANES), F32),
            pltpu.VMEM((TS, D), F32),
            pltpu.VMEM((TS, D), BF16),
            pltpu.VMEM((TS, 2 * D), F32),
            pltpu.VMEM((NCHUNK, TS, LANES), F32),
            pltpu.VMEM((3 * SUBLANES, D), F32),
            pltpu.VMEM((SUBLANES, D), F32),
            pltpu.VMEM((TS, TS), BF16),
            pltpu.VMEM((1, LANES), F32),
            pltpu.VMEM((TS, LANES), F32),
        ],
        compiler_params=pltpu.CompilerParams(
            dimension_semantics=("arbitrary",), vmem_limit_bytes=VMEM_LIMIT),
        name="rglru_layer",
    )(x, *consts)


def _shortconv_layer(h, win, cw, wout, lng, lnb, wr, br):
    n_tiles = h.shape[0] // TS
    out_shape, out_specs = _mixer_out(n_tiles)
    consts = (win, cw, wout, lng, lnb, wr, br)
    return pl.pallas_call(
        _shortconv_kernel,
        grid=(n_tiles + 1,),
        in_specs=[pl.BlockSpec((TS, D), lambda s: (jnp.minimum(s, n_tiles - 1), 0))]
        + [_const_spec(c.shape) for c in consts],
        out_specs=out_specs,
        out_shape=out_shape,
        scratch_shapes=[
            pltpu.VMEM((TS, 3 * D), F32),
            pltpu.VMEM((TS, D), F32),
            pltpu.VMEM((TS, D), BF16),
            pltpu.VMEM((SUBLANES, D), F32),
            pltpu.VMEM((TS, TS), BF16),
            pltpu.VMEM((1, LANES), F32),
            pltpu.VMEM((TS, LANES), F32),
        ],
        compiler_params=pltpu.CompilerParams(
            dimension_semantics=("arbitrary",), vmem_limit_bytes=VMEM_LIMIT),
        name="shortconv_layer",
    )(h, *consts)


def _tile_rows(t):
    return pl.ds(pl.multiple_of(t * SUBLANES, SUBLANES), SUBLANES)


def _dispatch_kernel(pend_ref, padded_ref, nused_ref, dest_ref, h_ref, xs_ref, zero_scr, sem_z, sem):
    i = pl.program_id(0)
    n_blocks = xs_ref.shape[0] // (BM * SUBLANES)

    def zero_block(blk):
        start = pl.multiple_of(blk * (BM * SUBLANES), BM * SUBLANES)
        return pltpu.make_async_copy(zero_scr, xs_ref.at[pl.ds(start, BM * SUBLANES), :], sem_z)

    @pl.when(i == 0)
    def _():
        zero_scr[...] = jnp.zeros_like(zero_scr)
        for e in range(N_EXPERTS):
            @pl.when(padded_ref[e] > 0)
            def _():
                zero_block(pend_ref[e] // BM - 1).start()

        def fill_tail(blk, c):
            zero_block(blk).start()
            return c

        lax.fori_loop(nused_ref[0], n_blocks, fill_tail, 0)
        for e in range(N_EXPERTS):
            @pl.when(padded_ref[e] > 0)
            def _():
                zero_block(0).wait()

        def drain_tail(blk, c):
            zero_block(0).wait()
            return c

        lax.fori_loop(nused_ref[0], n_blocks, drain_tail, 0)

    def row_copy(t, d):
        return pltpu.make_async_copy(h_ref.at[_tile_rows(t), :], xs_ref.at[_tile_rows(d), :], sem)

    for jj in range(TTM // TS):
        def issue(i8, c, jj=jj):
            for u in range(ISSUE_UNROLL):
                tl = i8 * ISSUE_UNROLL + u
                for k in range(2):
                    row_copy(jj * TS + tl, dest_ref[0, 0, (2 * jj + k) * TS + tl]).start(priority=k)
            return c

        lax.fori_loop(0, TS // ISSUE_UNROLL, issue, 0)

    def drain(i8, c):
        for _ in range(2 * ISSUE_UNROLL):
            row_copy(0, 0).wait()
        return c

    lax.fori_loop(0, TTM // ISSUE_UNROLL, drain, 0)


def _dispatch(h1t, dest_m, pend, padded, n_used, n_rows):
    nt = dest_m.shape[0]
    return pl.pallas_call(
        _dispatch_kernel,
        grid_spec=pltpu.PrefetchScalarGridSpec(
            num_scalar_prefetch=3,
            grid=(nt,),
            in_specs=[pl.BlockSpec((1, 1, 2 * TTM), lambda i, *_: (i, 0, 0), memory_space=pltpu.SMEM),
                      pl.BlockSpec((TTM * SUBLANES, LANES), lambda i, *_: (i, 0))],
            out_specs=pl.BlockSpec(memory_space=pl.ANY),
            scratch_shapes=[pltpu.VMEM((BM * SUBLANES, LANES), F32),
                            pltpu.SemaphoreType.DMA(()),
                            pltpu.SemaphoreType.DMA(())],
        ),
        out_shape=jax.ShapeDtypeStruct((n_rows * SUBLANES, LANES), F32),
        compiler_params=pltpu.CompilerParams(dimension_semantics=("arbitrary",)),
        name="moe_dispatch",
    )(pend, padded, n_used, dest_m, h1t)


def _expert_kernel(be_ref, nu_ref, xs_ref, wg_ref, wu_ref, wd_ref, ys_ref, wg_b, wu_b, wd_b):
    i = pl.program_id(0)

    @pl.when(i < nu_ref[0])
    def _():
        @pl.when((i == 0) | (be_ref[i] != be_ref[jnp.maximum(i - 1, 0)]))
        def _():
            wg_b[...] = wg_ref[0, 0].astype(BF16)
            wu_b[...] = wu_ref[0, 0].astype(BF16)
            wd_b[...] = wd_ref[0, 0].astype(BF16)

        xb = _load_token_tiles(xs_ref, 0, BM).astype(BF16)
        g = jnp.dot(xb, wg_b[...], preferred_element_type=F32)
        u = jnp.dot(xb, wu_b[...], preferred_element_type=F32)
        hmid = (jax.nn.silu(g) * u).astype(BF16)
        _store_token_tiles(ys_ref, 0, jnp.dot(hmid, wd_b[...], preferred_element_type=F32))

    @pl.when(i >= nu_ref[0])
    def _():
        ys_ref[...] = jnp.zeros_like(ys_ref)


def _experts(xs, block_expert, n_used, wg, wu, wd, layer):
    blk_rows = BM * SUBLANES
    nb = xs.shape[0] // blk_rows
    return pl.pallas_call(
        _expert_kernel,
        grid_spec=pltpu.PrefetchScalarGridSpec(
            num_scalar_prefetch=2,
            grid=(nb,),
            in_specs=[pl.BlockSpec((blk_rows, LANES), lambda i, be, nu: (jnp.minimum(i, nu[0] - 1), 0)),
                      pl.BlockSpec((1, 1, D, D_EXPERT), lambda i, be, nu: (layer, be[i], 0, 0)),
                      pl.BlockSpec((1, 1, D, D_EXPERT), lambda i, be, nu: (layer, be[i], 0, 0)),
                      pl.BlockSpec((1, 1, D_EXPERT, D), lambda i, be, nu: (layer, be[i], 0, 0))],
            out_specs=pl.BlockSpec((blk_rows, LANES), lambda i, be, nu: (i, 0)),
            scratch_shapes=[pltpu.VMEM((D, D_EXPERT), BF16),
                            pltpu.VMEM((D, D_EXPERT), BF16),
                            pltpu.VMEM((D_EXPERT, D), BF16)],
        ),
        out_shape=jax.ShapeDtypeStruct(xs.shape, F32),
        compiler_params=pltpu.CompilerParams(
            dimension_semantics=("arbitrary",), vmem_limit_bytes=VMEM_LIMIT),
        name="moe_experts",
    )(block_expert, n_used, xs, wg, wu, wd, )


def _combine_kernel(dest_ref, h_ref, route_ref, lng_ref, lnb_ref, ys_ref, out_ref, y0_scr, y1_scr, sems,
                    *, half, lc, stride_half, stride_k):
    out2d = out_ref if len(out_ref.shape) == 2 else out_ref.at[0]

    def row_copy(d, buf, t, sem):
        return pltpu.make_async_copy(ys_ref.at[_tile_rows(d), :], buf.at[_tile_rows(t), :], sem)

    for hf in range(2):
        def issue(i8, c, hf=hf):
            for u in range(ISSUE_UNROLL):
                tl = i8 * ISSUE_UNROLL + u
                base = hf * stride_half + tl
                row_copy(dest_ref[0, 0, base], y0_scr, hf * half + tl, sems.at[hf]).start(priority=0)
                row_copy(dest_ref[0, 0, base + stride_k], y1_scr, hf * half + tl, sems.at[hf]).start(priority=1)
            return c

        lax.fori_loop(0, half // ISSUE_UNROLL, issue, 0)

    g = lng_ref[...]
    bta = lnb_ref[...]

    for hf in range(2):
        def drain(i8, c, hf=hf):
            for _ in range(2 * ISSUE_UNROLL):
                row_copy(0, y0_scr, 0, sems.at[hf]).wait()
            return c

        lax.fori_loop(0, half // ISSUE_UNROLL, drain, 0)

        for r in range(hf * half, (hf + 1) * half, lc):
            g1 = route_ref[r:r + lc, 4:5]
            g2 = route_ref[r:r + lc, 5:6]
            y = g1 * _load_token_tiles(y0_scr, r, lc) + g2 * _load_token_tiles(y1_scr, r, lc)
            z = ALPHA * _load_token_tiles(h_ref, r, lc) + y
            out2d[r:r + lc, :] = _layer_norm_rows(z, g, bta)


def _combine_scratch(tt):
    return [pltpu.VMEM((tt * SUBLANES, LANES), F32),
            pltpu.VMEM((tt * SUBLANES, LANES), F32),
            pltpu.SemaphoreType.DMA((2,))]


def _combine(h1t, route, dest_m, ys, lng, lnb):
    nt = dest_m.shape[0]
    body = functools.partial(_combine_kernel, half=TS, lc=LCC, stride_half=2 * TS, stride_k=TS)
    return pl.pallas_call(
        body,
        grid=(nt,),
        in_specs=[pl.BlockSpec((1, 1, 2 * TTM), lambda i: (i, 0, 0), memory_space=pltpu.SMEM),
                  pl.BlockSpec((TTM * SUBLANES, LANES), lambda i: (i, 0)),
                  pl.BlockSpec((TTM, LANES), lambda i: (i, 0)),
                  pl.BlockSpec((1, D), lambda i: (0, 0)),
                  pl.BlockSpec((1, D), lambda i: (0, 0)),
                  pl.BlockSpec(memory_space=pl.ANY)],
        out_specs=pl.BlockSpec((TTM, D), lambda i: (i, 0)),
        out_shape=jax.ShapeDtypeStruct((nt * TTM, D), F32),
        scratch_shapes=_combine_scratch(TTM),
        compiler_params=pltpu.CompilerParams(dimension_semantics=("arbitrary",)),
        name="moe_combine",
    )(dest_m, h1t, route, lng, lnb, ys)


def _combine_final(h1t, route, dest_f, ys, lng, lnb, batch):
    per_seq = SEQ // TF

    def tok0(i):
        return pl.multiple_of((i // per_seq) * SEQ_TOT + N_META_TOK + (i % per_seq) * TF, SUBLANES)

    body = functools.partial(_combine_kernel, half=TF // 2, lc=LCF, stride_half=TF // 2, stride_k=TF)
    return pl.pallas_call(
        body,
        grid=(batch * per_seq,),
        in_specs=[pl.BlockSpec((1, 1, 2 * TF), lambda i: (i, 0, 0), memory_space=pltpu.SMEM),
                  pl.BlockSpec((pl.Element(TF * SUBLANES), pl.Element(LANES)),
                               lambda i: (pl.multiple_of(tok0(i) * SUBLANES, SUBLANES * SUBLANES), 0)),
                  pl.BlockSpec((pl.Element(TF), pl.Element(LANES)), lambda i: (tok0(i), 0)),
                  pl.BlockSpec((1, D), lambda i: (0, 0)),
                  pl.BlockSpec((1, D), lambda i: (0, 0)),
                  pl.BlockSpec(memory_space=pl.ANY)],
        out_specs=pl.BlockSpec((1, TF, D), lambda i: (i // per_seq, i % per_seq, 0)),
        out_shape=jax.ShapeDtypeStruct((batch, SEQ, D), F32),
        scratch_shapes=_combine_scratch(TF),
        compiler_params=pltpu.CompilerParams(dimension_semantics=("arbitrary",)),
        name="moe_combine_final",
    )(dest_f, h1t, route, lng, lnb, ys)


def _moe(h1t, route, rt, cnt, wg, wu, wd, layer, lng, lnb, batch, final):
    t_tot = route.shape[0]
    nj = t_tot // (batch * TS)
    n_blocks = (2 * t_tot + N_EXPERTS * (BM - 1) + BM - 1) // BM
    rt = rt.reshape(batch, nj, SUBLANES, TS)
    eid = rt[:, :, 0:2, :].astype(jnp.int32)
    rank = rt[:, :, 2:4, :].astype(jnp.int32)
    counts = cnt[0, :N_EXPERTS].astype(jnp.int32)
    padded = (counts + BM - 1) // BM * BM
    pend = jnp.cumsum(padded)
    dest = rank
    for e in range(1, N_EXPERTS):
        dest = dest + jnp.where(eid >= e, padded[e - 1], 0)
    dest_m = dest.reshape(batch * nj * TS // TTM, 1, 2 * TTM)
    n_used = pend[-1:] // BM
    blk = jnp.minimum(jnp.arange(n_blocks, dtype=jnp.int32), n_used[0] - 1) * BM
    block_expert = jnp.minimum(jnp.sum((pend[None, :] <= blk[:, None]).astype(jnp.int32), axis=1),
                               N_EXPERTS - 1)

    xs = _dispatch(h1t, dest_m, pend, padded, n_used, n_blocks * BM)
    ys = _experts(xs, block_expert, n_used, wg, wu, wd, layer)
    if not final:
        return _combine(h1t, route, dest_m, ys, lng, lnb)
    dest_f = dest.transpose(0, 2, 1, 3).reshape(batch, 2, SEQ_TOT)[:, :, N_META_TOK:]
    dest_f = dest_f.reshape(batch, 2, SEQ // TF, TF).transpose(0, 2, 1, 3).reshape(batch * (SEQ // TF), 1, 2 * TF)
    return _combine_final(h1t, route, dest_f, ys, lng, lnb, batch)


def _block_diag(w):
    per = GATE_BLK // HEAD_DIM
    w4 = w.reshape(D // GATE_BLK, per, HEAD_DIM, HEAD_DIM)
    out = jnp.zeros((D // GATE_BLK, GATE_BLK, GATE_BLK), w.dtype)
    for p in range(per):
        out = out.at[:, p * HEAD_DIM:(p + 1) * HEAD_DIM, p * HEAD_DIM:(p + 1) * HEAD_DIM].set(w4[:, p])
    return out


def _router_params(w_group, b_group, w_expert, b_expert):
    pad = LANES - N_GROUPS - N_EXPERTS
    wr = jnp.concatenate([w_group, w_expert, jnp.zeros((D, pad), F32)], axis=1).astype(BF16)
    br = jnp.concatenate([b_group, b_expert.reshape(-1), jnp.zeros((pad,), F32)])[None, :]
    return wr, br


def kernel(x, meta_tokens, lru_w_in, lru_conv_w, lru_conv_b, lru_w_a, lru_b_a, lru_w_i, lru_b_i, lru_lambda, lru_w_out, sc_w_in, sc_conv_w, sc_w_out, moe_w_group, moe_b_group, moe_w_expert, moe_b_expert, moe_w_gate, moe_w_up, moe_w_down, ln_g, ln_b):
    batch = x.shape[0]
    row = lambda v: v.reshape(1, -1)

    wr, br = _router_params(moe_w_group[0], moe_b_group[0], moe_w_expert[0], moe_b_expert[0])
    h1, route, rt, cnt = _rglru_layer(
        x, meta_tokens.astype(x.dtype), lru_w_in[0].astype(BF16), lru_conv_w[0], row(lru_conv_b[0]),
        _block_diag(lru_w_a[0]).astype(BF16), _block_diag(lru_w_i[0]).astype(BF16),
        row(lru_b_a[0]), row(lru_b_i[0]), row(lru_lambda[0]), lru_w_out[0].astype(BF16),
        row(ln_g[0, 0]), row(ln_b[0, 0]), wr, br)
    h2 = _moe(h1, route, rt, cnt, moe_w_gate, moe_w_up, moe_w_down, 0,
              row(ln_g[0, 1]), row(ln_b[0, 1]), batch, final=False)

    wr, br = _router_params(moe_w_group[1], moe_b_group[1], moe_w_expert[1], moe_b_expert[1])
    h3, route, rt, cnt = _shortconv_layer(
        h2, sc_w_in[0].astype(BF16), sc_conv_w[0], sc_w_out[0].astype(BF16),
        row(ln_g[1, 0]), row(ln_b[1, 0]), wr, br)
    return _moe(h3, route, rt, cnt, moe_w_gate, moe_w_up, moe_w_down, 1,
                row(ln_g[1, 1]), row(ln_b[1, 1]), batch, final=True)
```

```python
import functools

import jax
import jax.numpy as jnp
from jax import lax
from jax.experimental import pallas as pl
from jax.experimental.pallas import tpu as pltpu

F32 = jnp.float32
BF16 = jnp.bfloat16

D = 1024
SEQ = 8192
N_META_TOK = 16
SEQ_TOT = SEQ + N_META_TOK
LRU_HEADS = 16
HEAD_DIM = D // LRU_HEADS
LRU_C = 8.0
N_GROUPS = 4
EPG = 8
N_EXPERTS = N_GROUPS * EPG
D_EXPERT = D // 2
DEPTH = 2
ALPHA = (2.0 * DEPTH) ** 0.25
LN_EPS = 1e-5

LANES = 128
SUBLANES = 8
GATE_BLK = 256

TS = 432
SEG = TS // SUBLANES
TTM = 2 * TS
TF = 1024
BM = 512
CH = 16
LC = 144
LCC = 216
LCF = 256
WALK_UNROLL = 9
ISSUE_UNROLL = 16
NCHUNK = D // LANES

VMEM_LIMIT = 56 * 1024 * 1024


def _softplus(x):
    return jnp.maximum(x, 0.0) + jnp.log1p(jnp.exp(-jnp.abs(x)))


def _sigmoid(x):
    return 0.5 * jnp.tanh(0.5 * x) + 0.5


def _load_token_tiles(ref, r, n):
    return jnp.concatenate(
        [ref[pl.ds(r * SUBLANES + c, n, stride=SUBLANES), :] for c in range(NCHUNK)], axis=1)


def _store_token_tiles(ref, r, val):
    n = val.shape[0]
    for c in range(NCHUNK):
        ref[pl.ds(r * SUBLANES + c, n, stride=SUBLANES), :] = val[:, c * LANES:(c + 1) * LANES]


def _layer_norm_rows(z, g, bta):
    mu = jnp.mean(z, axis=-1, keepdims=True)
    zc = z - mu
    var = jnp.mean(zc * zc, axis=-1, keepdims=True)
    return zc * lax.rsqrt(var + LN_EPS) * g + bta


def _shifted(cur, prv, s, sub):
    return jnp.where(sub >= s, pltpu.roll(cur, s, 0), pltpu.roll(prv, s, 0))


def _exact_transpose_8(slab):
    sel = jnp.where(lax.broadcasted_iota(jnp.int32, (SUBLANES, LANES), 0)
                    == lax.broadcasted_iota(jnp.int32, (SUBLANES, LANES), 1), 1.0, 0.0).astype(BF16)
    p1 = slab.astype(BF16)
    r1 = slab - p1.astype(F32)
    p2 = r1.astype(BF16)
    p3 = (r1 - p2.astype(F32)).astype(BF16)
    dn = (((1,), (1,)), ((), ()))
    out = lax.dot_general(sel, p1, dn, preferred_element_type=F32)
    out = out + lax.dot_general(sel, p2, dn, preferred_element_type=F32)
    return out + lax.dot_general(sel, p3, dn, preferred_element_type=F32)


def _route_init(tri_scr, cnt_scr, logits_scr):
    ts = tri_scr.shape[0]
    row = lax.broadcasted_iota(jnp.int32, (ts, ts), 0)
    col = lax.broadcasted_iota(jnp.int32, (ts, ts), 1)
    tri_scr[...] = jnp.where(col < row, 1.0, 0.0).astype(BF16)
    cnt_scr[...] = jnp.zeros_like(cnt_scr)
    logits_scr[...] = jnp.zeros_like(logits_scr)


def _ln_logits(h2d, m_scr, lng_ref, lnb_ref, wr_ref, br_ref, h1_tiles, h1b_scr, logits_scr):
    ts = m_scr.shape[0]
    g = lng_ref[...]
    bta = lnb_ref[...]

    def ln_body(i, c):
        r = pl.multiple_of(i * LC, LC)
        z = ALPHA * h2d[pl.ds(r, LC), :] + m_scr[pl.ds(r, LC), :]
        y = _layer_norm_rows(z, g, bta)
        _store_token_tiles(h1_tiles, r, y)
        h1b_scr[pl.ds(r, LC), :] = y.astype(BF16)
        return c

    lax.fori_loop(0, ts // LC, ln_body, 0)

    logits_scr[...] = jnp.dot(h1b_scr[...], wr_ref[...], preferred_element_type=F32) + br_ref[...]


def _route(logits_scr, tri_scr, cnt_scr, route_ref, rt_ref, cnt_ref, valid):
    ts = logits_scr.shape[0]
    logits = logits_scr[...]
    lane = lax.broadcasted_iota(jnp.int32, (ts, LANES), 1).astype(F32)
    neg = jnp.float32(-jnp.inf)
    big = jnp.float32(LANES)

    gmask = lane < N_GROUPS
    gl = jnp.where(gmask, logits, neg)
    gmax = jnp.max(gl, axis=-1, keepdims=True)
    gidx = jnp.min(jnp.where(gl == gmax, lane, big), axis=-1, keepdims=True)
    sumexp = jnp.sum(jnp.where(gmask, jnp.exp(gl - gmax), 0.0), axis=-1, keepdims=True)
    pg = 1.0 / sumexp

    lo = N_GROUPS + EPG * gidx
    emask = (lane >= lo) & (lane < lo + EPG)
    el = jnp.where(emask, logits, neg)
    v1 = jnp.max(el, axis=-1, keepdims=True)
    i1 = jnp.min(jnp.where(el == v1, lane, big), axis=-1, keepdims=True)
    el2 = jnp.where(lane == i1, neg, el)
    v2 = jnp.max(el2, axis=-1, keepdims=True)
    i2 = jnp.min(jnp.where(el2 == v2, lane, big), axis=-1, keepdims=True)
    ex = jnp.exp(v2 - v1)
    den = 1.0 / (1.0 + ex)
    g1 = pg * den
    g2 = pg * ex * den
    e1 = i1 - N_GROUPS
    e2 = i2 - N_GROUPS

    sel1 = lane == e1
    sel2 = lane == e2
    onehot = jnp.where(sel1 | sel2, 1.0, 0.0)
    before = jnp.dot(tri_scr[...], onehot.astype(BF16), preferred_element_type=F32) + cnt_scr[...]
    rank1 = jnp.sum(jnp.where(sel1, before, 0.0), axis=-1, keepdims=True)
    rank2 = jnp.sum(jnp.where(sel2, before, 0.0), axis=-1, keepdims=True)
    cnt_scr[...] = cnt_scr[...] + valid * jnp.sum(onehot, axis=0, keepdims=True)

    slab = jnp.where(lane == 0, e1, 0.0)
    slab = jnp.where(lane == 1, e2, slab)
    slab = jnp.where(lane == 2, rank1, slab)
    slab = jnp.where(lane == 3, rank2, slab)
    slab = jnp.where(lane == 4, g1, slab)
    slab = jnp.where(lane == 5, g2, slab)
    route_ref[...] = slab
    rt_ref[0] = _exact_transpose_8(slab)
    cnt_ref[...] = cnt_scr[...]


def _seg_load(ref3, c0, g):
    return jnp.concatenate(
        [ref3[c0 + c, pl.ds(g, SUBLANES, stride=SEG), :] for c in range(NCHUNK)], axis=1)


def _seg_store(ref3, g, val):
    for c in range(NCHUNK):
        ref3[c, pl.ds(g, SUBLANES, stride=SEG), :] = val[:, c * LANES:(c + 1) * LANES]


def _rglru_kernel(x_ref, meta_ref, win_ref, cw_ref, cb_ref, wa_ref, wi_ref, ba_ref, bi_ref, lam_ref,
                  wout_ref, lng_ref, lnb_ref, wr_ref, br_ref,
                  h1_ref, route_ref, rt_ref, cnt_ref,
                  h_scr, u_scr, xc_scr, xcb_scr, g_scr, y_scr, halo_scr, carry_scr, tri_scr, cnt_scr,
                  logits_scr):
    s = pl.program_id(0)
    j = s % (SEQ_TOT // TS)

    @pl.when(s == 0)
    def _():
        _route_init(tri_scr, cnt_scr, logits_scr)

    @pl.when(j == 0)
    def _():
        halo_scr[...] = jnp.zeros_like(halo_scr)
        carry_scr[...] = jnp.zeros_like(carry_scr)
        h_scr[0:N_META_TOK, :] = meta_ref[...]
        h_scr[N_META_TOK:TS, :] = x_ref[0, 0:TS - N_META_TOK, :]

    @pl.when(j > 0)
    def _():
        h_scr[...] = x_ref[0]

    u = jnp.dot(h_scr[...].astype(BF16), win_ref[...], preferred_element_type=F32)
    for c in range(2 * NCHUNK):
        u_scr[c] = u[:, c * LANES:(c + 1) * LANES]

    _route(logits_scr, tri_scr, cnt_scr, route_ref, rt_ref, cnt_ref, jnp.where(s > 0, 1.0, 0.0))

    sub = lax.broadcasted_iota(jnp.int32, (SUBLANES, D), 0)
    cw = [cw_ref[k:k + 1, :] for k in range(4)]
    cb = cb_ref[...]

    tails = []
    for m in range(3):
        um = _seg_load(u_scr, NCHUNK, SEG - 3 + m)
        tails.append(jnp.where(sub >= 1, pltpu.roll(um, 1, 0),
                               pltpu.roll(halo_scr[m * SUBLANES:(m + 1) * SUBLANES, :], 1, 0)))
        halo_scr[m * SUBLANES:(m + 1) * SUBLANES, :] = um

    def conv_body(i, carry):
        x1, x2, x3 = carry
        outs = []
        for hh in range(CH // SUBLANES):
            cur = _seg_load(u_scr, NCHUNK, 2 * i + hh)
            outs.append(cur * cw[3] + x1 * cw[2] + x2 * cw[1] + x3 * cw[0] + cb)
            x1, x2, x3 = cur, x1, x2
        xc = jnp.concatenate(outs, axis=0)
        r = pl.multiple_of(i * CH, CH)
        xc_scr[pl.ds(r, CH), :] = xc
        xcb_scr[pl.ds(r, CH), :] = xc.astype(BF16)
        return x1, x2, x3

    lax.fori_loop(0, TS // CH, conv_body, (tails[2], tails[1], tails[0]))

    for k in range(D // GATE_BLK):
        xk = xcb_scr[:, k * GATE_BLK:(k + 1) * GATE_BLK]
        g_scr[:, k * GATE_BLK:(k + 1) * GATE_BLK] = jnp.dot(xk, wa_ref[k], preferred_element_type=F32)
        g_scr[:, D + k * GATE_BLK:D + (k + 1) * GATE_BLK] = jnp.dot(xk, wi_ref[k], preferred_element_type=F32)

    ba = ba_ref[...]
    bi = bi_ref[...]
    nsp = -LRU_C * _softplus(-lam_ref[...])

    def scan_body(i, carry):
        hl, pc = carry
        for hh in range(WALK_UNROLL):
            rows = pl.ds(pl.multiple_of((i * WALK_UNROLL + hh) * SUBLANES, SUBLANES), SUBLANES)
            xc = xc_scr[rows, :]
            ra = _sigmoid(g_scr[rows, :D] + ba)
            gi = _sigmoid(g_scr[rows, D:] + bi)
            log_a = ra * nsp
            a = jnp.exp(log_a)
            m2 = -jnp.tanh(log_a) * (a * a + 1.0)
            hl = a * hl + jnp.sqrt(m2) * (gi * xc)
            pc = a * pc
            g_scr[rows, :D] = hl
            g_scr[rows, D:] = pc
        return hl, pc

    hl_end, pc_end = lax.fori_loop(0, SEG // WALK_UNROLL, scan_body,
                                   (jnp.zeros((SUBLANES, D), F32), jnp.ones((SUBLANES, D), F32)))

    c0 = carry_scr[...]
    av, bv = pc_end, hl_end
    for sh in (1, 2, 4):
        m = sub >= sh
        a_s = pltpu.roll(av, sh, 0)
        b_s = pltpu.roll(bv, sh, 0)
        bv = jnp.where(m, av * b_s + bv, bv)
        av = jnp.where(m, av * a_s, av)
    h_end = av * c0 + bv
    cin = jnp.where(sub >= 1, pltpu.roll(h_end, 1, 0), c0)
    carry_scr[...] = jnp.broadcast_to(h_end[SUBLANES - 1:SUBLANES, :], (SUBLANES, D))

    def fix_body(i, c):
        for hh in range(WALK_UNROLL):
            g = i * WALK_UNROLL + hh
            rows = pl.ds(pl.multiple_of(g * SUBLANES, SUBLANES), SUBLANES)
            hs = g_scr[rows, :D] + g_scr[rows, D:] * cin
            _seg_store(y_scr, g, hs * jax.nn.gelu(_seg_load(u_scr, 0, g)))
        return c

    lax.fori_loop(0, SEG // WALK_UNROLL, fix_body, 0)

    y = jnp.concatenate([y_scr[c] for c in range(NCHUNK)], axis=1).astype(BF16)
    xc_scr[...] = jnp.dot(y, wout_ref[...], preferred_element_type=F32)
    _ln_logits(h_scr, xc_scr, lng_ref, lnb_ref, wr_ref, br_ref, h1_ref, xcb_scr, logits_scr)


def _shortconv_kernel(h_ref, win_ref, cw_ref, wout_ref, lng_ref, lnb_ref, wr_ref, br_ref,
                      h1_ref, route_ref, rt_ref, cnt_ref,
                      u_scr, m_scr, y_scr, halo_scr, tri_scr, cnt_scr, logits_scr):
    s = pl.program_id(0)
    j = s % (SEQ_TOT // TS)
    ts = m_scr.shape[0]

    @pl.when(s == 0)
    def _():
        _route_init(tri_scr, cnt_scr, logits_scr)

    @pl.when(j == 0)
    def _():
        halo_scr[...] = jnp.zeros_like(halo_scr)

    u_scr[...] = jnp.dot(h_ref[...].astype(BF16), win_ref[...], preferred_element_type=F32)

    _route(logits_scr, tri_scr, cnt_scr, route_ref, rt_ref, cnt_ref, jnp.where(s > 0, 1.0, 0.0))

    sub = lax.broadcasted_iota(jnp.int32, (SUBLANES, D), 0)
    cw = [cw_ref[k:k + 1, :] for k in range(3)]

    def conv_body(i, prv):
        r = pl.multiple_of(i * CH, CH)
        ys = []
        for hh in range(CH // SUBLANES):
            r0 = r + SUBLANES * hh
            cur = u_scr[pl.ds(r0, SUBLANES), D:2 * D] * u_scr[pl.ds(r0, SUBLANES), 2 * D:]
            acc = cur * cw[2]
            for sh in (1, 2):
                acc = acc + _shifted(cur, prv, sh, sub) * cw[2 - sh]
            ys.append(u_scr[pl.ds(r0, SUBLANES), :D] * acc)
            prv = cur
        y_scr[pl.ds(r, CH), :] = jnp.concatenate(ys, axis=0).astype(BF16)
        return prv

    halo_scr[...] = lax.fori_loop(0, ts // CH, conv_body, halo_scr[...])

    m_scr[...] = jnp.dot(y_scr[...], wout_ref[...], preferred_element_type=F32)
    _ln_logits(h_ref, m_scr, lng_ref, lnb_ref, wr_ref, br_ref, h1_ref, y_scr, logits_scr)


def _const_spec(shape):
    n = len(shape)
    return pl.BlockSpec(shape, lambda s: (0,) * n)


def _mixer_out(n_tiles):
    out_shape = (jax.ShapeDtypeStruct(((n_tiles + 1) * TS * SUBLANES, LANES), F32),
                 jax.ShapeDtypeStruct((n_tiles * TS, LANES), F32),
                 jax.ShapeDtypeStruct((n_tiles, SUBLANES, TS), F32),
                 jax.ShapeDtypeStruct((1, LANES), F32))
    prev = lambda s: jnp.maximum(s - 1, 0)
    out_specs = (pl.BlockSpec((TS * SUBLANES, LANES), lambda s: (s, 0)),
                 pl.BlockSpec((TS, LANES), lambda s: (prev(s), 0)),
                 pl.BlockSpec((1, SUBLANES, TS), lambda s: (prev(s), 0, 0)),
                 pl.BlockSpec((1, LANES), lambda s: (0, 0)))
    return out_shape, out_specs


def _rglru_layer(x, meta, win, cw, cb, wa, wi, ba, bi, lam, wout, lng, lnb, wr, br):
    batch = x.shape[0]
    nj = SEQ_TOT // TS
    n_tiles = batch * nj
    out_shape, out_specs = _mixer_out(n_tiles)
    consts = (meta, win, cw, cb, wa, wi, ba, bi, lam, wout, lng, lnb, wr, br)

    def x_index(s):
        t = jnp.minimum(s, n_tiles - 1)
        return (t // nj, pl.multiple_of(jnp.maximum((t % nj) * TS - N_META_TOK, 0), SUBLANES), 0)

    x_spec = pl.BlockSpec((pl.Element(1), pl.Element(TS), pl.Element(D)), x_index)
    return pl.pallas_call(
        _rglru_kernel,
        grid=(n_tiles + 1,),
        in_specs=[x_spec] + [_const_spec(c.shape) for c in consts],
        out_specs=out_specs,
        out_shape=out_shape,
        scratch_shapes=[
            pltpu.VMEM((TS, D), F32),
            pltpu.VMEM((2 * NCHUNK, TS, LANES), F32),
            pltpu.VMEM((TS, D), F32),
            pltpu.VMEM((TS, D), BF16),
            pltpu.VMEM((TS, 2 * D), F32),
            pltpu.VMEM((NCHUNK, TS, LANES), F32),
            pltpu.VMEM((3 * SUBLANES, D), F32),
            pltpu.VMEM((SUBLANES, D), F32),
            pltpu.VMEM((TS, TS), BF16),
            pltpu.VMEM((1, LANES), F32),
            pltpu.VMEM((TS, LANES), F32),
        ],
        compiler_params=pltpu.CompilerParams(
            dimension_semantics=("arbitrary",), vmem_limit_bytes=VMEM_LIMIT),
        name="rglru_layer",
    )(x, *consts)


def _shortconv_layer(h, win, cw, wout, lng, lnb, wr, br):
    n_tiles = h.shape[0] // TS
    out_shape, out_specs = _mixer_out(n_tiles)
    consts = (win, cw, wout, lng, lnb, wr, br)
    return pl.pallas_call(
        _shortconv_kernel,
        grid=(n_tiles + 1,),
        in_specs=[pl.BlockSpec((TS, D), lambda s: (jnp.minimum(s, n_tiles - 1), 0))]
        + [_const_spec(c.shape) for c in consts],
        out_specs=out_specs,
        out_shape=out_shape,
        scratch_shapes=[
            pltpu.VMEM((TS, 3 * D), F32),
            pltpu.VMEM((TS, D), F32),
            pltpu.VMEM((TS, D), BF16),
            pltpu.VMEM((SUBLANES, D), F32),
            pltpu.VMEM((TS, TS), BF16),
            pltpu.VMEM((1, LANES), F32),
            pltpu.VMEM((TS, LANES), F32),
        ],
        compiler_params=pltpu.CompilerParams(
            dimension_semantics=("arbitrary",), vmem_limit_bytes=VMEM_LIMIT),
        name="shortconv_layer",
    )(h, *consts)


def _tile_rows(t):
    return pl.ds(pl.multiple_of(t * SUBLANES, SUBLANES), SUBLANES)


def _dispatch_kernel(pend_ref, padded_ref, nused_ref, dest_ref, h_ref, xs_ref, zero_scr, sem_z, sem):
    i = pl.program_id(0)
    n_blocks = xs_ref.shape[0] // (BM * SUBLANES)

    def zero_block(blk):
        start = pl.multiple_of(blk * (BM * SUBLANES), BM * SUBLANES)
        return pltpu.make_async_copy(zero_scr, xs_ref.at[pl.ds(start, BM * SUBLANES), :], sem_z)

    @pl.when(i == 0)
    def _():
        zero_scr[...] = jnp.zeros_like(zero_scr)
        for e in range(N_EXPERTS):
            @pl.when(padded_ref[e] > 0)
            def _():
                zero_block(pend_ref[e] // BM - 1).start()

        def fill_tail(blk, c):
            zero_block(blk).start()
            return c

        lax.fori_loop(nused_ref[0], n_blocks, fill_tail, 0)
        for e in range(N_EXPERTS):
            @pl.when(padded_ref[e] > 0)
            def _():
                zero_block(0).wait()

        def drain_tail(blk, c):
            zero_block(0).wait()
            return c

        lax.fori_loop(nused_ref[0], n_blocks, drain_tail, 0)

    def row_copy(t, d):
        return pltpu.make_async_copy(h_ref.at[_tile_rows(t), :], xs_ref.at[_tile_rows(d), :], sem)

    for jj in range(TTM // TS):
        def issue(i8, c, jj=jj):
            for u in range(ISSUE_UNROLL):
                tl = i8 * ISSUE_UNROLL + u
                for k in range(2):
                    row_copy(jj * TS + tl, dest_ref[0, 0, (2 * jj + k) * TS + tl]).start(priority=k)
            return c

        lax.fori_loop(0, TS // ISSUE_UNROLL, issue, 0)

    def drain(i8, c):
        for _ in range(2 * ISSUE_UNROLL):
            row_copy(0, 0).wait()
        return c

    lax.fori_loop(0, TTM // ISSUE_UNROLL, drain, 0)


def _dispatch(h1t, dest_m, pend, padded, n_used, n_rows):
    nt = dest_m.shape[0]
    return pl.pallas_call(
        _dispatch_kernel,
        grid_spec=pltpu.PrefetchScalarGridSpec(
            num_scalar_prefetch=3,
            grid=(nt,),
            in_specs=[pl.BlockSpec((1, 1, 2 * TTM), lambda i, *_: (i, 0, 0), memory_space=pltpu.SMEM),
                      pl.BlockSpec((TTM * SUBLANES, LANES), lambda i, *_: (i, 0))],
            out_specs=pl.BlockSpec(memory_space=pl.ANY),
            scratch_shapes=[pltpu.VMEM((BM * SUBLANES, LANES), F32),
                            pltpu.SemaphoreType.DMA(()),
                            pltpu.SemaphoreType.DMA(())],
        ),
        out_shape=jax.ShapeDtypeStruct((n_rows * SUBLANES, LANES), F32),
        compiler_params=pltpu.CompilerParams(dimension_semantics=("arbitrary",)),
        name="moe_dispatch",
    )(pend, padded, n_used, dest_m, h1t)


def _expert_kernel(be_ref, nu_ref, xs_ref, wg_ref, wu_ref, wd_ref, ys_ref, wg_b, wu_b, wd_b):
    i = pl.program_id(0)

    @pl.when(i < nu_ref[0])
    def _():
        @pl.when((i == 0) | (be_ref[i] != be_ref[jnp.maximum(i - 1, 0)]))
        def _():
            wg_b[...] = wg_ref[0, 0].astype(BF16)
            wu_b[...] = wu_ref[0, 0].astype(BF16)
            wd_b[...] = wd_ref[0, 0].astype(BF16)

        xb = _load_token_tiles(xs_ref, 0, BM).astype(BF16)
        g = jnp.dot(xb, wg_b[...], preferred_element_type=F32)
        u = jnp.dot(xb, wu_b[...], preferred_element_type=F32)
        hmid = (jax.nn.silu(g) * u).astype(BF16)
        _store_token_tiles(ys_ref, 0, jnp.dot(hmid, wd_b[...], preferred_element_type=F32))

    @pl.when(i >= nu_ref[0])
    def _():
        ys_ref[...] = jnp.zeros_like(ys_ref)


def _experts(xs, block_expert, n_used, wg, wu, wd, layer):
    blk_rows = BM * SUBLANES
    nb = xs.shape[0] // blk_rows
    return pl.pallas_call(
        _expert_kernel,
        grid_spec=pltpu.PrefetchScalarGridSpec(
            num_scalar_prefetch=2,
            grid=(nb,),
            in_specs=[pl.BlockSpec((blk_rows, LANES), lambda i, be, nu: (jnp.minimum(i, nu[0] - 1), 0)),
                      pl.BlockSpec((1, 1, D, D_EXPERT), lambda i, be, nu: (layer, be[i], 0, 0)),
                      pl.BlockSpec((1, 1, D, D_EXPERT), lambda i, be, nu: (layer, be[i], 0, 0)),
                      pl.BlockSpec((1, 1, D_EXPERT, D), lambda i, be, nu: (layer, be[i], 0, 0))],
            out_specs=pl.BlockSpec((blk_rows, LANES), lambda i, be, nu: (i, 0)),
            scratch_shapes=[pltpu.VMEM((D, D_EXPERT), BF16),
                            pltpu.VMEM((D, D_EXPERT), BF16),
                            pltpu.VMEM((D_EXPERT, D), BF16)],
        ),
        out_shape=jax.ShapeDtypeStruct(xs.shape, F32),
        compiler_params=pltpu.CompilerParams(
            dimension_semantics=("arbitrary",), vmem_limit_bytes=VMEM_LIMIT),
        name="moe_experts",
    )(block_expert, n_used, xs, wg, wu, wd, )


def _combine_kernel(dest_ref, dest_next_ref, h_ref, route_ref, lng_ref, lnb_ref, ys_ref, out_ref,
                    y0_scr, y1_scr, sems, *, half, lc, stride_half, stride_k):
    i = pl.program_id(0)
    n_steps = pl.num_programs(0)
    slot = i % 2
    out2d = out_ref if len(out_ref.shape) == 2 else out_ref.at[0]

    def row_copy(d, buf, sl, t):
        return pltpu.make_async_copy(ys_ref.at[_tile_rows(d), :], buf.at[sl, _tile_rows(t), :], sems.at[sl])

    def issue_tile(dref, sl):
        for hf in range(2):
            def issue(i8, c, hf=hf):
                for u in range(ISSUE_UNROLL):
                    tl = i8 * ISSUE_UNROLL + u
                    base = hf * stride_half + tl
                    row_copy(dref[0, 0, base], y0_scr, sl, hf * half + tl).start(priority=0)
                    row_copy(dref[0, 0, base + stride_k], y1_scr, sl, hf * half + tl).start(priority=1)
                return c

            lax.fori_loop(0, half // ISSUE_UNROLL, issue, 0)

    @pl.when(i == 0)
    def _():
        issue_tile(dest_ref, slot)

    @pl.when(i + 1 < n_steps)
    def _():
        issue_tile(dest_next_ref, 1 - slot)

    def drain(i8, c):
        for _ in range(2 * ISSUE_UNROLL):
            row_copy(0, y0_scr, slot, 0).wait()
        return c

    lax.fori_loop(0, 2 * half // ISSUE_UNROLL, drain, 0)

    g = lng_ref[...]
    bta = lnb_ref[...]
    y0_cur = y0_scr.at[slot]
    y1_cur = y1_scr.at[slot]
    for r in range(0, 2 * half, lc):
        g1 = route_ref[r:r + lc, 4:5]
        g2 = route_ref[r:r + lc, 5:6]
        y = g1 * _load_token_tiles(y0_cur, r, lc) + g2 * _load_token_tiles(y1_cur, r, lc)
        z = ALPHA * _load_token_tiles(h_ref, r, lc) + y
        out2d[r:r + lc, :] = _layer_norm_rows(z, g, bta)


def _combine_scratch(tt):
    return [pltpu.VMEM((2, tt * SUBLANES, LANES), F32),
            pltpu.VMEM((2, tt * SUBLANES, LANES), F32),
            pltpu.SemaphoreType.DMA((2,))]


def _combine(h1t, route, dest_m, ys, lng, lnb):
    nt = dest_m.shape[0]
    body = functools.partial(_combine_kernel, half=TS, lc=LCC, stride_half=2 * TS, stride_k=TS)
    return pl.pallas_call(
        body,
        grid=(nt,),
        in_specs=[pl.BlockSpec((1, 1, 2 * TTM), lambda i: (i, 0, 0), memory_space=pltpu.SMEM),
                  pl.BlockSpec((1, 1, 2 * TTM), lambda i: (jnp.minimum(i + 1, nt - 1), 0, 0),
                               memory_space=pltpu.SMEM),
                  pl.BlockSpec((TTM * SUBLANES, LANES), lambda i: (i, 0)),
                  pl.BlockSpec((TTM, LANES), lambda i: (i, 0)),
                  pl.BlockSpec((1, D), lambda i: (0, 0)),
                  pl.BlockSpec((1, D), lambda i: (0, 0)),
                  pl.BlockSpec(memory_space=pl.ANY)],
        out_specs=pl.BlockSpec((TTM, D), lambda i: (i, 0)),
        out_shape=jax.ShapeDtypeStruct((nt * TTM, D), F32),
        scratch_shapes=_combine_scratch(TTM),
        compiler_params=pltpu.CompilerParams(
            dimension_semantics=("arbitrary",), vmem_limit_bytes=VMEM_LIMIT),
        name="moe_combine",
    )(dest_m, dest_m, h1t, route, lng, lnb, ys)


def _combine_final(h1t, route, dest_f, ys, lng, lnb, batch):
    per_seq = SEQ // TF

    def tok0(i):
        return pl.multiple_of((i // per_seq) * SEQ_TOT + N_META_TOK + (i % per_seq) * TF, SUBLANES)

    body = functools.partial(_combine_kernel, half=TF // 2, lc=LCF, stride_half=TF // 2, stride_k=TF)
    return pl.pallas_call(
        body,
        grid=(batch * per_seq,),
        in_specs=[pl.BlockSpec((1, 1, 2 * TF), lambda i: (i, 0, 0), memory_space=pltpu.SMEM),
                  pl.BlockSpec((1, 1, 2 * TF), lambda i: (jnp.minimum(i + 1, batch * per_seq - 1), 0, 0),
                               memory_space=pltpu.SMEM),
                  pl.BlockSpec((pl.Element(TF * SUBLANES), pl.Element(LANES)),
                               lambda i: (pl.multiple_of(tok0(i) * SUBLANES, SUBLANES * SUBLANES), 0)),
                  pl.BlockSpec((pl.Element(TF), pl.Element(LANES)), lambda i: (tok0(i), 0)),
                  pl.BlockSpec((1, D), lambda i: (0, 0)),
                  pl.BlockSpec((1, D), lambda i: (0, 0)),
                  pl.BlockSpec(memory_space=pl.ANY)],
        out_specs=pl.BlockSpec((1, TF, D), lambda i: (i // per_seq, i % per_seq, 0)),
        out_shape=jax.ShapeDtypeStruct((batch, SEQ, D), F32),
        scratch_shapes=_combine_scratch(TF),
        compiler_params=pltpu.CompilerParams(
            dimension_semantics=("arbitrary",), vmem_limit_bytes=VMEM_LIMIT),
        name="moe_combine_final",
    )(dest_f, dest_f, h1t, route, lng, lnb, ys)


def _moe(h1t, route, rt, cnt, wg, wu, wd, layer, lng, lnb, batch, final):
    t_tot = route.shape[0]
    nj = t_tot // (batch * TS)
    n_blocks = (2 * t_tot + N_EXPERTS * (BM - 1) + BM - 1) // BM
    rt = rt.reshape(batch, nj, SUBLANES, TS)
    eid = rt[:, :, 0:2, :].astype(jnp.int32)
    rank = rt[:, :, 2:4, :].astype(jnp.int32)
    counts = cnt[0, :N_EXPERTS].astype(jnp.int32)
    padded = (counts + BM - 1) // BM * BM
    pend = jnp.cumsum(padded)
    dest = rank
    for e in range(1, N_EXPERTS):
        dest = dest + jnp.where(eid >= e, padded[e - 1], 0)
    dest_m = dest.reshape(batch * nj * TS // TTM, 1, 2 * TTM)
    n_used = pend[-1:] // BM
    blk = jnp.minimum(jnp.arange(n_blocks, dtype=jnp.int32), n_used[0] - 1) * BM
    block_expert = jnp.minimum(jnp.sum((pend[None, :] <= blk[:, None]).astype(jnp.int32), axis=1),
                               N_EXPERTS - 1)

    xs = _dispatch(h1t, dest_m, pend, padded, n_used, n_blocks * BM)
    ys = _experts(xs, block_expert, n_used, wg, wu, wd, layer)
    if not final:
        return _combine(h1t, route, dest_m, ys, lng, lnb)
    dest_f = dest.transpose(0, 2, 1, 3).reshape(batch, 2, SEQ_TOT)[:, :, N_META_TOK:]
    dest_f = dest_f.reshape(batch, 2, SEQ // TF, TF).transpose(0, 2, 1, 3).reshape(batch * (SEQ // TF), 1, 2 * TF)
    return _combine_final(h1t, route, dest_f, ys, lng, lnb, batch)


def _block_diag(w):
    per = GATE_BLK // HEAD_DIM
    w4 = w.reshape(D // GATE_BLK, per, HEAD_DIM, HEAD_DIM)
    out = jnp.zeros((D // GATE_BLK, GATE_BLK, GATE_BLK), w.dtype)
    for p in range(per):
        out = out.at[:, p * HEAD_DIM:(p + 1) * HEAD_DIM, p * HEAD_DIM:(p + 1) * HEAD_DIM].set(w4[:, p])
    return out


def _router_params(w_group, b_group, w_expert, b_expert):
    pad = LANES - N_GROUPS - N_EXPERTS
    wr = jnp.concatenate([w_group, w_expert, jnp.zeros((D, pad), F32)], axis=1).astype(BF16)
    br = jnp.concatenate([b_group, b_expert.reshape(-1), jnp.zeros((pad,), F32)])[None, :]
    return wr, br


def kernel(x, meta_tokens, lru_w_in, lru_conv_w, lru_conv_b, lru_w_a, lru_b_a, lru_w_i, lru_b_i, lru_lambda, lru_w_out, sc_w_in, sc_conv_w, sc_w_out, moe_w_group, moe_b_group, moe_w_expert, moe_b_expert, moe_w_gate, moe_w_up, moe_w_down, ln_g, ln_b):
    batch = x.shape[0]
    row = lambda v: v.reshape(1, -1)

    wr, br = _router_params(moe_w_group[0], moe_b_group[0], moe_w_expert[0], moe_b_expert[0])
    h1, route, rt, cnt = _rglru_layer(
        x, meta_tokens.astype(x.dtype), lru_w_in[0].astype(BF16), lru_conv_w[0], row(lru_conv_b[0]),
        _block_diag(lru_w_a[0]).astype(BF16), _block_diag(lru_w_i[0]).astype(BF16),
        row(lru_b_a[0]), row(lru_b_i[0]), row(lru_lambda[0]), lru_w_out[0].astype(BF16),
        row(ln_g[0, 0]), row(ln_b[0, 0]), wr, br)
    h2 = _moe(h1, route, rt, cnt, moe_w_gate, moe_w_up, moe_w_down, 0,
              row(ln_g[0, 1]), row(ln_b[0, 1]), batch, final=False)

    wr, br = _router_params(moe_w_group[1], moe_b_group[1], moe_w_expert[1], moe_b_expert[1])
    h3, route, rt, cnt = _shortconv_layer(
        h2, sc_w_in[0].astype(BF16), sc_conv_w[0], sc_w_out[0].astype(BF16),
        row(ln_g[1, 0]), row(ln_b[1, 0]), wr, br)
    return _moe(h3, route, rt, cnt, moe_w_gate, moe_w_up, moe_w_down, 1,
                row(ln_g[1, 1]), row(ln_b[1, 1]), batch, final=True)
```

```python
import functools

import jax
import jax.numpy as jnp
from jax import lax
from jax.experimental import pallas as pl
from jax.experimental.pallas import tpu as pltpu

F32 = jnp.float32
BF16 = jnp.bfloat16

D = 1024
SEQ = 8192
N_META_TOK = 16
SEQ_TOT = SEQ + N_META_TOK
LRU_HEADS = 16
HEAD_DIM = D // LRU_HEADS
LRU_C = 8.0
N_GROUPS = 4
EPG = 8
N_EXPERTS = N_GROUPS * EPG
D_EXPERT = D // 2
DEPTH = 2
ALPHA = (2.0 * DEPTH) ** 0.25
LN_EPS = 1e-5

LANES = 128
SUBLANES = 8
GATE_BLK = 256

TS = 432
SEG = TS // SUBLANES
TTM = 2 * TS
TF = 1024
BM = 512
CH = 16
LC = 144
LCC = 216
LCF = 256
WALK_UNROLL = 9
ISSUE_UNROLL = 16
NCHUNK = D // LANES

VMEM_LIMIT = 56 * 1024 * 1024


def _softplus(x):
    return jnp.maximum(x, 0.0) + jnp.log1p(jnp.exp(-jnp.abs(x)))


def _sigmoid(x):
    return 0.5 * jnp.tanh(0.5 * x) + 0.5


def _load_token_tiles(ref, r, n):
    return jnp.concatenate(
        [ref[pl.ds(r * SUBLANES + c, n, stride=SUBLANES), :] for c in range(NCHUNK)], axis=1)


def _store_token_tiles(ref, r, val):
    n = val.shape[0]
    for c in range(NCHUNK):
        ref[pl.ds(r * SUBLANES + c, n, stride=SUBLANES), :] = val[:, c * LANES:(c + 1) * LANES]


def _layer_norm_rows(z, g, bta):
    mu = jnp.mean(z, axis=-1, keepdims=True)
    zc = z - mu
    var = jnp.mean(zc * zc, axis=-1, keepdims=True)
    return zc * lax.rsqrt(var + LN_EPS) * g + bta


def _shifted(cur, prv, s, sub):
    return jnp.where(sub >= s, pltpu.roll(cur, s, 0), pltpu.roll(prv, s, 0))


def _exact_transpose_8(slab):
    sel = jnp.where(lax.broadcasted_iota(jnp.int32, (SUBLANES, LANES), 0)
                    == lax.broadcasted_iota(jnp.int32, (SUBLANES, LANES), 1), 1.0, 0.0).astype(BF16)
    p1 = slab.astype(BF16)
    r1 = slab - p1.astype(F32)
    p2 = r1.astype(BF16)
    p3 = (r1 - p2.astype(F32)).astype(BF16)
    dn = (((1,), (1,)), ((), ()))
    out = lax.dot_general(sel, p1, dn, preferred_element_type=F32)
    out = out + lax.dot_general(sel, p2, dn, preferred_element_type=F32)
    return out + lax.dot_general(sel, p3, dn, preferred_element_type=F32)


def _route_init(tri_scr, cnt_scr, logits_scr):
    ts = tri_scr.shape[0]
    row = lax.broadcasted_iota(jnp.int32, (ts, ts), 0)
    col = lax.broadcasted_iota(jnp.int32, (ts, ts), 1)
    tri_scr[...] = jnp.where(col < row, 1.0, 0.0).astype(BF16)
    cnt_scr[...] = jnp.zeros_like(cnt_scr)
    logits_scr[...] = jnp.zeros_like(logits_scr)


def _ln_logits(h2d, m_scr, lng_ref, lnb_ref, wr_ref, br_ref, h1_tiles, h1b_scr, logits_scr):
    ts = m_scr.shape[0]
    g = lng_ref[...]
    bta = lnb_ref[...]

    def ln_body(i, c):
        r = pl.multiple_of(i * LC, LC)
        z = ALPHA * h2d[pl.ds(r, LC), :] + m_scr[pl.ds(r, LC), :]
        y = _layer_norm_rows(z, g, bta)
        _store_token_tiles(h1_tiles, r, y)
        h1b_scr[pl.ds(r, LC), :] = y.astype(BF16)
        return c

    lax.fori_loop(0, ts // LC, ln_body, 0)

    logits_scr[...] = jnp.dot(h1b_scr[...], wr_ref[...], preferred_element_type=F32) + br_ref[...]


def _route(logits_scr, tri_scr, cnt_scr, route_ref, rt_ref, cnt_ref, valid):
    ts = logits_scr.shape[0]
    logits = logits_scr[...]
    lane = lax.broadcasted_iota(jnp.int32, (ts, LANES), 1).astype(F32)
    neg = jnp.float32(-jnp.inf)
    big = jnp.float32(LANES)

    gmask = lane < N_GROUPS
    gl = jnp.where(gmask, logits, neg)
    gmax = jnp.max(gl, axis=-1, keepdims=True)
    gidx = jnp.min(jnp.where(gl == gmax, lane, big), axis=-1, keepdims=True)
    sumexp = jnp.sum(jnp.where(gmask, jnp.exp(gl - gmax), 0.0), axis=-1, keepdims=True)
    pg = 1.0 / sumexp

    lo = N_GROUPS + EPG * gidx
    emask = (lane >= lo) & (lane < lo + EPG)
    el = jnp.where(emask, logits, neg)
    v1 = jnp.max(el, axis=-1, keepdims=True)
    i1 = jnp.min(jnp.where(el == v1, lane, big), axis=-1, keepdims=True)
    el2 = jnp.where(lane == i1, neg, el)
    v2 = jnp.max(el2, axis=-1, keepdims=True)
    i2 = jnp.min(jnp.where(el2 == v2, lane, big), axis=-1, keepdims=True)
    ex = jnp.exp(v2 - v1)
    den = 1.0 / (1.0 + ex)
    g1 = pg * den
    g2 = pg * ex * den
    e1 = i1 - N_GROUPS
    e2 = i2 - N_GROUPS

    sel1 = lane == e1
    sel2 = lane == e2
    onehot = jnp.where(sel1 | sel2, 1.0, 0.0)
    before = jnp.dot(tri_scr[...], onehot.astype(BF16), preferred_element_type=F32) + cnt_scr[...]
    rank1 = jnp.sum(jnp.where(sel1, before, 0.0), axis=-1, keepdims=True)
    rank2 = jnp.sum(jnp.where(sel2, before, 0.0), axis=-1, keepdims=True)
    cnt_scr[...] = cnt_scr[...] + valid * jnp.sum(onehot, axis=0, keepdims=True)

    slab = jnp.where(lane == 0, e1, 0.0)
    slab = jnp.where(lane == 1, e2, slab)
    slab = jnp.where(lane == 2, rank1, slab)
    slab = jnp.where(lane == 3, rank2, slab)
    slab = jnp.where(lane == 4, g1, slab)
    slab = jnp.where(lane == 5, g2, slab)
    route_ref[...] = slab
    rt_ref[0] = _exact_transpose_8(slab)
    cnt_ref[...] = cnt_scr[...]


def _seg_load(ref3, c0, g):
    return jnp.concatenate(
        [ref3[c0 + c, pl.ds(g, SUBLANES, stride=SEG), :] for c in range(NCHUNK)], axis=1)


def _seg_store(ref3, g, val):
    for c in range(NCHUNK):
        ref3[c, pl.ds(g, SUBLANES, stride=SEG), :] = val[:, c * LANES:(c + 1) * LANES]


def _rglru_kernel(x_ref, meta_ref, win_ref, cw_ref, cb_ref, wa_ref, wi_ref, ba_ref, bi_ref, lam_ref,
                  wout_ref, lng_ref, lnb_ref, wr_ref, br_ref,
                  h1_ref, route_ref, rt_ref, cnt_ref,
                  h_scr, u_scr, xc_scr, xcb_scr, g_scr, y_scr, halo_scr, carry_scr, tri_scr, cnt_scr,
                  logits_scr):
    s = pl.program_id(0)
    j = s % (SEQ_TOT // TS)

    @pl.when(s == 0)
    def _():
        _route_init(tri_scr, cnt_scr, logits_scr)

    @pl.when(j == 0)
    def _():
        halo_scr[...] = jnp.zeros_like(halo_scr)
        carry_scr[...] = jnp.zeros_like(carry_scr)
        h_scr[0:N_META_TOK, :] = meta_ref[...]
        h_scr[N_META_TOK:TS, :] = x_ref[0, 0:TS - N_META_TOK, :]

    @pl.when(j > 0)
    def _():
        h_scr[...] = x_ref[0]

    u = jnp.dot(h_scr[...].astype(BF16), win_ref[...], preferred_element_type=F32)
    for c in range(2 * NCHUNK):
        u_scr[c] = u[:, c * LANES:(c + 1) * LANES]

    _route(logits_scr, tri_scr, cnt_scr, route_ref, rt_ref, cnt_ref, jnp.where(s > 0, 1.0, 0.0))

    sub = lax.broadcasted_iota(jnp.int32, (SUBLANES, D), 0)
    cw = [cw_ref[k:k + 1, :] for k in range(4)]
    cb = cb_ref[...]

    tails = []
    for m in range(3):
        um = _seg_load(u_scr, NCHUNK, SEG - 3 + m)
        tails.append(jnp.where(sub >= 1, pltpu.roll(um, 1, 0),
                               pltpu.roll(halo_scr[m * SUBLANES:(m + 1) * SUBLANES, :], 1, 0)))
        halo_scr[m * SUBLANES:(m + 1) * SUBLANES, :] = um

    def conv_body(i, carry):
        x1, x2, x3 = carry
        outs = []
        for hh in range(CH // SUBLANES):
            cur = _seg_load(u_scr, NCHUNK, 2 * i + hh)
            outs.append(cur * cw[3] + x1 * cw[2] + x2 * cw[1] + x3 * cw[0] + cb)
            x1, x2, x3 = cur, x1, x2
        xc = jnp.concatenate(outs, axis=0)
        r = pl.multiple_of(i * CH, CH)
        xc_scr[pl.ds(r, CH), :] = xc
        xcb_scr[pl.ds(r, CH), :] = xc.astype(BF16)
        return x1, x2, x3

    lax.fori_loop(0, TS // CH, conv_body, (tails[2], tails[1], tails[0]))

    for k in range(D // GATE_BLK):
        xk = xcb_scr[:, k * GATE_BLK:(k + 1) * GATE_BLK]
        g_scr[:, k * GATE_BLK:(k + 1) * GATE_BLK] = jnp.dot(xk, wa_ref[k], preferred_element_type=F32)
        g_scr[:, D + k * GATE_BLK:D + (k + 1) * GATE_BLK] = jnp.dot(xk, wi_ref[k], preferred_element_type=F32)

    ba = ba_ref[...]
    bi = bi_ref[...]
    nsp = -LRU_C * _softplus(-lam_ref[...])

    def scan_body(i, carry):
        hl, pc = carry
        for hh in range(WALK_UNROLL):
            rows = pl.ds(pl.multiple_of((i * WALK_UNROLL + hh) * SUBLANES, SUBLANES), SUBLANES)
            xc = xc_scr[rows, :]
            ra = _sigmoid(g_scr[rows, :D] + ba)
            gi = _sigmoid(g_scr[rows, D:] + bi)
            log_a = ra * nsp
            a = jnp.exp(log_a)
            m2 = -jnp.tanh(log_a) * (a * a + 1.0)
            hl = a * hl + jnp.sqrt(m2) * (gi * xc)
            pc = a * pc
            g_scr[rows, :D] = hl
            g_scr[rows, D:] = pc
        return hl, pc

    hl_end, pc_end = lax.fori_loop(0, SEG // WALK_UNROLL, scan_body,
                                   (jnp.zeros((SUBLANES, D), F32), jnp.ones((SUBLANES, D), F32)))

    c0 = carry_scr[...]
    av, bv = pc_end, hl_end
    for sh in (1, 2, 4):
        m = sub >= sh
        a_s = pltpu.roll(av, sh, 0)
        b_s = pltpu.roll(bv, sh, 0)
        bv = jnp.where(m, av * b_s + bv, bv)
        av = jnp.where(m, av * a_s, av)
    h_end = av * c0 + bv
    cin = jnp.where(sub >= 1, pltpu.roll(h_end, 1, 0), c0)
    carry_scr[...] = jnp.broadcast_to(h_end[SUBLANES - 1:SUBLANES, :], (SUBLANES, D))

    def fix_body(i, c):
        for hh in range(WALK_UNROLL):
            g = i * WALK_UNROLL + hh
            rows = pl.ds(pl.multiple_of(g * SUBLANES, SUBLANES), SUBLANES)
            hs = g_scr[rows, :D] + g_scr[rows, D:] * cin
            _seg_store(y_scr, g, hs * jax.nn.gelu(_seg_load(u_scr, 0, g)))
        return c

    lax.fori_loop(0, SEG // WALK_UNROLL, fix_body, 0)

    y = jnp.concatenate([y_scr[c] for c in range(NCHUNK)], axis=1).astype(BF16)
    xc_scr[...] = jnp.dot(y, wout_ref[...], preferred_element_type=F32)
    _ln_logits(h_scr, xc_scr, lng_ref, lnb_ref, wr_ref, br_ref, h1_ref, xcb_scr, logits_scr)


def _shortconv_kernel(h_ref, win_ref, cw_ref, wout_ref, lng_ref, lnb_ref, wr_ref, br_ref,
                      h1_ref, route_ref, rt_ref, cnt_ref,
                      u_scr, m_scr, y_scr, halo_scr, tri_scr, cnt_scr, logits_scr):
    s = pl.program_id(0)
    j = s % (SEQ_TOT // TS)
    ts = m_scr.shape[0]

    @pl.when(s == 0)
    def _():
        _route_init(tri_scr, cnt_scr, logits_scr)

    @pl.when(j == 0)
    def _():
        halo_scr[...] = jnp.zeros_like(halo_scr)

    u_scr[...] = jnp.dot(h_ref[...].astype(BF16), win_ref[...], preferred_element_type=F32)

    _route(logits_scr, tri_scr, cnt_scr, route_ref, rt_ref, cnt_ref, jnp.where(s > 0, 1.0, 0.0))

    sub = lax.broadcasted_iota(jnp.int32, (SUBLANES, D), 0)
    cw = [cw_ref[k:k + 1, :] for k in range(3)]

    def conv_body(i, prv):
        r = pl.multiple_of(i * CH, CH)
        ys = []
        for hh in range(CH // SUBLANES):
            r0 = r + SUBLANES * hh
            cur = u_scr[pl.ds(r0, SUBLANES), D:2 * D] * u_scr[pl.ds(r0, SUBLANES), 2 * D:]
            acc = cur * cw[2]
            for sh in (1, 2):
                acc = acc + _shifted(cur, prv, sh, sub) * cw[2 - sh]
            ys.append(u_scr[pl.ds(r0, SUBLANES), :D] * acc)
            prv = cur
        y_scr[pl.ds(r, CH), :] = jnp.concatenate(ys, axis=0).astype(BF16)
        return prv

    halo_scr[...] = lax.fori_loop(0, ts // CH, conv_body, halo_scr[...])

    m_scr[...] = jnp.dot(y_scr[...], wout_ref[...], preferred_element_type=F32)
    _ln_logits(h_ref, m_scr, lng_ref, lnb_ref, wr_ref, br_ref, h1_ref, y_scr, logits_scr)


def _const_spec(shape):
    n = len(shape)
    return pl.BlockSpec(shape, lambda s: (0,) * n)


def _mixer_out(n_tiles):
    out_shape = (jax.ShapeDtypeStruct(((n_tiles + 1) * TS * SUBLANES, LANES), F32),
                 jax.ShapeDtypeStruct((n_tiles * TS, LANES), F32),
                 jax.ShapeDtypeStruct((n_tiles, SUBLANES, TS), F32),
                 jax.ShapeDtypeStruct((1, LANES), F32))
    prev = lambda s: jnp.maximum(s - 1, 0)
    out_specs = (pl.BlockSpec((TS * SUBLANES, LANES), lambda s: (s, 0)),
                 pl.BlockSpec((TS, LANES), lambda s: (prev(s), 0)),
                 pl.BlockSpec((1, SUBLANES, TS), lambda s: (prev(s), 0, 0)),
                 pl.BlockSpec((1, LANES), lambda s: (0, 0)))
    return out_shape, out_specs


def _rglru_layer(x, meta, win, cw, cb, wa, wi, ba, bi, lam, wout, lng, lnb, wr, br):
    batch = x.shape[0]
    nj = SEQ_TOT // TS
    n_tiles = batch * nj
    out_shape, out_specs = _mixer_out(n_tiles)
    consts = (meta, win, cw, cb, wa, wi, ba, bi, lam, wout, lng, lnb, wr, br)

    def x_index(s):
        t = jnp.minimum(s, n_tiles - 1)
        return (t // nj, pl.multiple_of(jnp.maximum((t % nj) * TS - N_META_TOK, 0), SUBLANES), 0)

    x_spec = pl.BlockSpec((pl.Element(1), pl.Element(TS), pl.Element(D)), x_index)
    return pl.pallas_call(
        _rglru_kernel,
        grid=(n_tiles + 1,),
        in_specs=[x_spec] + [_const_spec(c.shape) for c in consts],
        out_specs=out_specs,
        out_shape=out_shape,
        scratch_shapes=[
            pltpu.VMEM((TS, D), F32),
            pltpu.VMEM((2 * NCHUNK, TS, LANES), F32),
            pltpu.VMEM((TS, D), F32),
            pltpu.VMEM((TS, D), BF16),
            pltpu.VMEM((TS, 2 * D), F32),
            pltpu.VMEM((NCHUNK, TS, LANES), F32),
            pltpu.VMEM((3 * SUBLANES, D), F32),
            pltpu.VMEM((SUBLANES, D), F32),
            pltpu.VMEM((TS, TS), BF16),
            pltpu.VMEM((1, LANES), F32),
            pltpu.VMEM((TS, LANES), F32),
        ],
        compiler_params=pltpu.CompilerParams(
            dimension_semantics=("arbitrary",), vmem_limit_bytes=VMEM_LIMIT),
        name="rglru_layer",
    )(x, *consts)


def _shortconv_layer(h, win, cw, wout, lng, lnb, wr, br):
    n_tiles = h.shape[0] // TS
    out_shape, out_specs = _mixer_out(n_tiles)
    consts = (win, cw, wout, lng, lnb, wr, br)
    return pl.pallas_call(
        _shortconv_kernel,
        grid=(n_tiles + 1,),
        in_specs=[pl.BlockSpec((TS, D), lambda s: (jnp.minimum(s, n_tiles - 1), 0))]
        + [_const_spec(c.shape) for c in consts],
        out_specs=out_specs,
        out_shape=out_shape,
        scratch_shapes=[
            pltpu.VMEM((TS, 3 * D), F32),
            pltpu.VMEM((TS, D), F32),
            pltpu.VMEM((TS, D), BF16),
            pltpu.VMEM((SUBLANES, D), F32),
            pltpu.VMEM((TS, TS), BF16),
            pltpu.VMEM((1, LANES), F32),
            pltpu.VMEM((TS, LANES), F32),
        ],
        compiler_params=pltpu.CompilerParams(
            dimension_semantics=("arbitrary",), vmem_limit_bytes=VMEM_LIMIT),
        name="shortconv_layer",
    )(h, *consts)


def _tile_rows(t):
    return pl.ds(pl.multiple_of(t * SUBLANES, SUBLANES), SUBLANES)


def _dispatch_kernel(pend_ref, padded_ref, nused_ref, dest_ref, h_ref, xs_ref, zero_scr, sem_z, sem):
    i = pl.program_id(0)
    n_blocks = xs_ref.shape[0] // (BM * SUBLANES)

    def zero_block(blk):
        start = pl.multiple_of(blk * (BM * SUBLANES), BM * SUBLANES)
        return pltpu.make_async_copy(zero_scr, xs_ref.at[pl.ds(start, BM * SUBLANES), :], sem_z)

    @pl.when(i == 0)
    def _():
        zero_scr[...] = jnp.zeros_like(zero_scr)
        for e in range(N_EXPERTS):
            @pl.when(padded_ref[e] > 0)
            def _():
                zero_block(pend_ref[e] // BM - 1).start()

        def fill_tail(blk, c):
            zero_block(blk).start()
            return c

        lax.fori_loop(nused_ref[0], n_blocks, fill_tail, 0)
        for e in range(N_EXPERTS):
            @pl.when(padded_ref[e] > 0)
            def _():
                zero_block(0).wait()

        def drain_tail(blk, c):
            zero_block(0).wait()
            return c

        lax.fori_loop(nused_ref[0], n_blocks, drain_tail, 0)

    def row_copy(t, d):
        return pltpu.make_async_copy(h_ref.at[_tile_rows(t), :], xs_ref.at[_tile_rows(d), :], sem)

    for jj in range(TTM // TS):
        def issue(i8, c, jj=jj):
            for u in range(ISSUE_UNROLL):
                tl = i8 * ISSUE_UNROLL + u
                for k in range(2):
                    row_copy(jj * TS + tl, dest_ref[0, 0, (2 * jj + k) * TS + tl]).start(priority=k)
            return c

        lax.fori_loop(0, TS // ISSUE_UNROLL, issue, 0)

    def drain(i8, c):
        for _ in range(2 * ISSUE_UNROLL):
            row_copy(0, 0).wait()
        return c

    lax.fori_loop(0, TTM // ISSUE_UNROLL, drain, 0)


def _dispatch(h1t, dest_m, pend, padded, n_used, n_rows):
    nt = dest_m.shape[0]
    return pl.pallas_call(
        _dispatch_kernel,
        grid_spec=pltpu.PrefetchScalarGridSpec(
            num_scalar_prefetch=3,
            grid=(nt,),
            in_specs=[pl.BlockSpec((1, 1, 2 * TTM), lambda i, *_: (i, 0, 0), memory_space=pltpu.SMEM),
                      pl.BlockSpec((TTM * SUBLANES, LANES), lambda i, *_: (i, 0))],
            out_specs=pl.BlockSpec(memory_space=pl.ANY),
            scratch_shapes=[pltpu.VMEM((BM * SUBLANES, LANES), F32),
                            pltpu.SemaphoreType.DMA(()),
                            pltpu.SemaphoreType.DMA(())],
        ),
        out_shape=jax.ShapeDtypeStruct((n_rows * SUBLANES, LANES), F32),
        compiler_params=pltpu.CompilerParams(dimension_semantics=("arbitrary",)),
        name="moe_dispatch",
    )(pend, padded, n_used, dest_m, h1t)


def _expert_kernel(be_ref, nu_ref, xs_ref, wg_ref, wu_ref, wd_ref, ys_ref, wg_b, wu_b, wd_b):
    i = pl.program_id(0)

    @pl.when(i < nu_ref[0])
    def _():
        @pl.when((i == 0) | (be_ref[i] != be_ref[jnp.maximum(i - 1, 0)]))
        def _():
            wg_b[...] = wg_ref[0, 0].astype(BF16)
            wu_b[...] = wu_ref[0, 0].astype(BF16)
            wd_b[...] = wd_ref[0, 0].astype(BF16)

        xb = _load_token_tiles(xs_ref, 0, BM).astype(BF16)
        g = jnp.dot(xb, wg_b[...], preferred_element_type=F32)
        u = jnp.dot(xb, wu_b[...], preferred_element_type=F32)
        hmid = (jax.nn.silu(g) * u).astype(BF16)
        _store_token_tiles(ys_ref, 0, jnp.dot(hmid, wd_b[...], preferred_element_type=F32))

    @pl.when(i >= nu_ref[0])
    def _():
        ys_ref[...] = jnp.zeros_like(ys_ref)


def _experts(xs, block_expert, n_used, wg, wu, wd, layer):
    blk_rows = BM * SUBLANES
    nb = xs.shape[0] // blk_rows
    return pl.pallas_call(
        _expert_kernel,
        grid_spec=pltpu.PrefetchScalarGridSpec(
            num_scalar_prefetch=2,
            grid=(nb,),
            in_specs=[pl.BlockSpec((blk_rows, LANES), lambda i, be, nu: (jnp.minimum(i, nu[0] - 1), 0)),
                      pl.BlockSpec((1, 1, D, D_EXPERT), lambda i, be, nu: (layer, be[i], 0, 0)),
                      pl.BlockSpec((1, 1, D, D_EXPERT), lambda i, be, nu: (layer, be[i], 0, 0)),
                      pl.BlockSpec((1, 1, D_EXPERT, D), lambda i, be, nu: (layer, be[i], 0, 0))],
            out_specs=pl.BlockSpec((blk_rows, LANES), lambda i, be, nu: (i, 0)),
            scratch_shapes=[pltpu.VMEM((D, D_EXPERT), BF16),
                            pltpu.VMEM((D, D_EXPERT), BF16),
                            pltpu.VMEM((D_EXPERT, D), BF16)],
        ),
        out_shape=jax.ShapeDtypeStruct(xs.shape, F32),
        compiler_params=pltpu.CompilerParams(
            dimension_semantics=("arbitrary",), vmem_limit_bytes=VMEM_LIMIT),
        name="moe_experts",
    )(block_expert, n_used, xs, wg, wu, wd, )


def _combine_kernel(dest_ref, dest_next_ref, h_ref, route_ref, lng_ref, lnb_ref, ys_ref, out_ref,
                    y0_scr, y1_scr, sems, *, half, lc, stride_half, stride_k):
    i = pl.program_id(0)
    n_steps = pl.num_programs(0)
    slot = i % 2
    out2d = out_ref if len(out_ref.shape) == 2 else out_ref.at[0]

    def row_copy(d, buf, sl, t):
        return pltpu.make_async_copy(ys_ref.at[_tile_rows(d), :], buf.at[sl, _tile_rows(t), :], sems.at[sl])

    n_it = half // ISSUE_UNROLL
    per_half = half // lc

    def issue_part(dref, sl, q):
        hf, p = divmod(q, per_half)

        def issue(i8, c):
            for u in range(ISSUE_UNROLL):
                tl = i8 * ISSUE_UNROLL + u
                base = hf * stride_half + tl
                row_copy(dref[0, 0, base], y0_scr, sl, hf * half + tl).start(priority=0)
                row_copy(dref[0, 0, base + stride_k], y1_scr, sl, hf * half + tl).start(priority=1)
            return c

        lax.fori_loop(p * n_it // per_half, (p + 1) * n_it // per_half, issue, 0)

    @pl.when(i == 0)
    def _():
        for q in range(2 * per_half):
            issue_part(dest_ref, slot, q)

    def drain(i8, c):
        for _ in range(2 * ISSUE_UNROLL):
            row_copy(0, y0_scr, slot, 0).wait()
        return c

    lax.fori_loop(0, 2 * n_it, drain, 0)

    g = lng_ref[...]
    bta = lnb_ref[...]
    y0_cur = y0_scr.at[slot]
    y1_cur = y1_scr.at[slot]
    for q, r in enumerate(range(0, 2 * half, lc)):
        @pl.when(i + 1 < n_steps)
        def _(q=q):
            issue_part(dest_next_ref, 1 - slot, q)

        g1 = route_ref[r:r + lc, 4:5]
        g2 = route_ref[r:r + lc, 5:6]
        y = g1 * _load_token_tiles(y0_cur, r, lc) + g2 * _load_token_tiles(y1_cur, r, lc)
        z = ALPHA * _load_token_tiles(h_ref, r, lc) + y
        out2d[r:r + lc, :] = _layer_norm_rows(z, g, bta)


def _combine_scratch(tt):
    return [pltpu.VMEM((2, tt * SUBLANES, LANES), F32),
            pltpu.VMEM((2, tt * SUBLANES, LANES), F32),
            pltpu.SemaphoreType.DMA((2,))]


def _combine(h1t, route, dest_m, ys, lng, lnb):
    nt = dest_m.shape[0]
    body = functools.partial(_combine_kernel, half=TS, lc=LCC, stride_half=2 * TS, stride_k=TS)
    return pl.pallas_call(
        body,
        grid=(nt,),
        in_specs=[pl.BlockSpec((1, 1, 2 * TTM), lambda i: (i, 0, 0), memory_space=pltpu.SMEM),
                  pl.BlockSpec((1, 1, 2 * TTM), lambda i: (jnp.minimum(i + 1, nt - 1), 0, 0),
                               memory_space=pltpu.SMEM),
                  pl.BlockSpec((TTM * SUBLANES, LANES), lambda i: (i, 0)),
                  pl.BlockSpec((TTM, LANES), lambda i: (i, 0)),
                  pl.BlockSpec((1, D), lambda i: (0, 0)),
                  pl.BlockSpec((1, D), lambda i: (0, 0)),
                  pl.BlockSpec(memory_space=pl.ANY)],
        out_specs=pl.BlockSpec((TTM, D), lambda i: (i, 0)),
        out_shape=jax.ShapeDtypeStruct((nt * TTM, D), F32),
        scratch_shapes=_combine_scratch(TTM),
        compiler_params=pltpu.CompilerParams(
            dimension_semantics=("arbitrary",), vmem_limit_bytes=VMEM_LIMIT),
        name="moe_combine",
    )(dest_m, dest_m, h1t, route, lng, lnb, ys)


def _combine_final(h1t, route, dest_f, ys, lng, lnb, batch):
    per_seq = SEQ // TF

    def tok0(i):
        return pl.multiple_of((i // per_seq) * SEQ_TOT + N_META_TOK + (i % per_seq) * TF, SUBLANES)

    body = functools.partial(_combine_kernel, half=TF // 2, lc=LCF, stride_half=TF // 2, stride_k=TF)
    return pl.pallas_call(
        body,
        grid=(batch * per_seq,),
        in_specs=[pl.BlockSpec((1, 1, 2 * TF), lambda i: (i, 0, 0), memory_space=pltpu.SMEM),
                  pl.BlockSpec((1, 1, 2 * TF), lambda i: (jnp.minimum(i + 1, batch * per_seq - 1), 0, 0),
                               memory_space=pltpu.SMEM),
                  pl.BlockSpec((pl.Element(TF * SUBLANES), pl.Element(LANES)),
                               lambda i: (pl.multiple_of(tok0(i) * SUBLANES, SUBLANES * SUBLANES), 0)),
                  pl.BlockSpec((pl.Element(TF), pl.Element(LANES)), lambda i: (tok0(i), 0)),
                  pl.BlockSpec((1, D), lambda i: (0, 0)),
                  pl.BlockSpec((1, D), lambda i: (0, 0)),
                  pl.BlockSpec(memory_space=pl.ANY)],
        out_specs=pl.BlockSpec((1, TF, D), lambda i: (i // per_seq, i % per_seq, 0)),
        out_shape=jax.ShapeDtypeStruct((batch, SEQ, D), F32),
        scratch_shapes=_combine_scratch(TF),
        compiler_params=pltpu.CompilerParams(
            dimension_semantics=("arbitrary",), vmem_limit_bytes=VMEM_LIMIT),
        name="moe_combine_final",
    )(dest_f, dest_f, h1t, route, lng, lnb, ys)


def _moe(h1t, route, rt, cnt, wg, wu, wd, layer, lng, lnb, batch, final):
    t_tot = route.shape[0]
    nj = t_tot // (batch * TS)
    n_blocks = (2 * t_tot + N_EXPERTS * (BM - 1) + BM - 1) // BM
    rt = rt.reshape(batch, nj, SUBLANES, TS)
    eid = rt[:, :, 0:2, :].astype(jnp.int32)
    rank = rt[:, :, 2:4, :].astype(jnp.int32)
    counts = cnt[0, :N_EXPERTS].astype(jnp.int32)
    padded = (counts + BM - 1) // BM * BM
    pend = jnp.cumsum(padded)
    dest = rank
    for e in range(1, N_EXPERTS):
        dest = dest + jnp.where(eid >= e, padded[e - 1], 0)
    dest_m = dest.reshape(batch * nj * TS // TTM, 1, 2 * TTM)
    n_used = pend[-1:] // BM
    blk = jnp.minimum(jnp.arange(n_blocks, dtype=jnp.int32), n_used[0] - 1) * BM
    block_expert = jnp.minimum(jnp.sum((pend[None, :] <= blk[:, None]).astype(jnp.int32), axis=1),
                               N_EXPERTS - 1)

    xs = _dispatch(h1t, dest_m, pend, padded, n_used, n_blocks * BM)
    ys = _experts(xs, block_expert, n_used, wg, wu, wd, layer)
    if not final:
        return _combine(h1t, route, dest_m, ys, lng, lnb)
    dest_f = dest.transpose(0, 2, 1, 3).reshape(batch, 2, SEQ_TOT)[:, :, N_META_TOK:]
    dest_f = dest_f.reshape(batch, 2, SEQ // TF, TF).transpose(0, 2, 1, 3).reshape(batch * (SEQ // TF), 1, 2 * TF)
    return _combine_final(h1t, route, dest_f, ys, lng, lnb, batch)


def _block_diag(w):
    per = GATE_BLK // HEAD_DIM
    w4 = w.reshape(D // GATE_BLK, per, HEAD_DIM, HEAD_DIM)
    out = jnp.zeros((D // GATE_BLK, GATE_BLK, GATE_BLK), w.dtype)
    for p in range(per):
        out = out.at[:, p * HEAD_DIM:(p + 1) * HEAD_DIM, p * HEAD_DIM:(p + 1) * HEAD_DIM].set(w4[:, p])
    return out


def _router_params(w_group, b_group, w_expert, b_expert):
    pad = LANES - N_GROUPS - N_EXPERTS
    wr = jnp.concatenate([w_group, w_expert, jnp.zeros((D, pad), F32)], axis=1).astype(BF16)
    br = jnp.concatenate([b_group, b_expert.reshape(-1), jnp.zeros((pad,), F32)])[None, :]
    return wr, br


def kernel(x, meta_tokens, lru_w_in, lru_conv_w, lru_conv_b, lru_w_a, lru_b_a, lru_w_i, lru_b_i, lru_lambda, lru_w_out, sc_w_in, sc_conv_w, sc_w_out, moe_w_group, moe_b_group, moe_w_expert, moe_b_expert, moe_w_gate, moe_w_up, moe_w_down, ln_g, ln_b):
    batch = x.shape[0]
    row = lambda v: v.reshape(1, -1)

    wr, br = _router_params(moe_w_group[0], moe_b_group[0], moe_w_expert[0], moe_b_expert[0])
    h1, route, rt, cnt = _rglru_layer(
        x, meta_tokens.astype(x.dtype), lru_w_in[0].astype(BF16), lru_conv_w[0], row(lru_conv_b[0]),
        _block_diag(lru_w_a[0]).astype(BF16), _block_diag(lru_w_i[0]).astype(BF16),
        row(lru_b_a[0]), row(lru_b_i[0]), row(lru_lambda[0]), lru_w_out[0].astype(BF16),
        row(ln_g[0, 0]), row(ln_b[0, 0]), wr, br)
    h2 = _moe(h1, route, rt, cnt, moe_w_gate, moe_w_up, moe_w_down, 0,
              row(ln_g[0, 1]), row(ln_b[0, 1]), batch, final=False)

    wr, br = _router_params(moe_w_group[1], moe_b_group[1], moe_w_expert[1], moe_b_expert[1])
    h3, route, rt, cnt = _shortconv_layer(
        h2, sc_w_in[0].astype(BF16), sc_conv_w[0], sc_w_out[0].astype(BF16),
        row(ln_g[1, 0]), row(ln_b[1, 0]), wr, br)
    return _moe(h3, route, rt, cnt, moe_w_gate, moe_w_up, moe_w_down, 1,
                row(ln_g[1, 1]), row(ln_b[1, 1]), batch, final=True)
```

```python
import functools

import jax
import jax.numpy as jnp
from jax import lax
from jax.experimental import pallas as pl
from jax.experimental.pallas import tpu as pltpu

F32 = jnp.float32
BF16 = jnp.bfloat16

D = 1024
SEQ = 8192
N_META_TOK = 16
SEQ_TOT = SEQ + N_META_TOK
LRU_HEADS = 16
HEAD_DIM = D // LRU_HEADS
LRU_C = 8.0
N_GROUPS = 4
EPG = 8
N_EXPERTS = N_GROUPS * EPG
D_EXPERT = D // 2
DEPTH = 2
ALPHA = (2.0 * DEPTH) ** 0.25
LN_EPS = 1e-5

LANES = 128
SUBLANES = 8
GATE_BLK = 256

TS = 432
SEG = TS // SUBLANES
TTM = 2 * TS
TF = 1024
BM = 512
CH = 16
LC = 144
LCC = 216
LCF = 256
WALK_UNROLL = 9
ISSUE_UNROLL = 16
NCHUNK = D // LANES

VMEM_LIMIT = 56 * 1024 * 1024


def _softplus(x):
    return jnp.maximum(x, 0.0) + jnp.log1p(jnp.exp(-jnp.abs(x)))


def _sigmoid(x):
    return 0.5 * jnp.tanh(0.5 * x) + 0.5


def _load_token_tiles(ref, r, n):
    return jnp.concatenate(
        [ref[pl.ds(r * SUBLANES + c, n, stride=SUBLANES), :] for c in range(NCHUNK)], axis=1)


def _store_token_tiles(ref, r, val):
    n = val.shape[0]
    for c in range(NCHUNK):
        ref[pl.ds(r * SUBLANES + c, n, stride=SUBLANES), :] = val[:, c * LANES:(c + 1) * LANES]


def _layer_norm_rows(z, g, bta):
    mu = jnp.mean(z, axis=-1, keepdims=True)
    zc = z - mu
    var = jnp.mean(zc * zc, axis=-1, keepdims=True)
    return zc * lax.rsqrt(var + LN_EPS) * g + bta


def _shifted(cur, prv, s, sub):
    return jnp.where(sub >= s, pltpu.roll(cur, s, 0), pltpu.roll(prv, s, 0))


def _exact_transpose_8(slab):
    sel = jnp.where(lax.broadcasted_iota(jnp.int32, (SUBLANES, LANES), 0)
                    == lax.broadcasted_iota(jnp.int32, (SUBLANES, LANES), 1), 1.0, 0.0).astype(BF16)
    p1 = slab.astype(BF16)
    r1 = slab - p1.astype(F32)
    p2 = r1.astype(BF16)
    p3 = (r1 - p2.astype(F32)).astype(BF16)
    dn = (((1,), (1,)), ((), ()))
    out = lax.dot_general(sel, p1, dn, preferred_element_type=F32)
    out = out + lax.dot_general(sel, p2, dn, preferred_element_type=F32)
    return out + lax.dot_general(sel, p3, dn, preferred_element_type=F32)


def _route_init(tri_scr, cnt_scr, logits_scr):
    ts = tri_scr.shape[0]
    row = lax.broadcasted_iota(jnp.int32, (ts, ts), 0)
    col = lax.broadcasted_iota(jnp.int32, (ts, ts), 1)
    tri_scr[...] = jnp.where(col < row, 1.0, 0.0).astype(BF16)
    cnt_scr[...] = jnp.zeros_like(cnt_scr)
    logits_scr[...] = jnp.zeros_like(logits_scr)


def _ln_logits(h2d, m_scr, lng_ref, lnb_ref, wr_ref, br_ref, h1_tiles, h1b_scr, logits_scr):
    ts = m_scr.shape[0]
    g = lng_ref[...]
    bta = lnb_ref[...]

    def ln_body(i, c):
        r = pl.multiple_of(i * LC, LC)
        z = ALPHA * h2d[pl.ds(r, LC), :] + m_scr[pl.ds(r, LC), :]
        y = _layer_norm_rows(z, g, bta)
        _store_token_tiles(h1_tiles, r, y)
        h1b_scr[pl.ds(r, LC), :] = y.astype(BF16)
        return c

    lax.fori_loop(0, ts // LC, ln_body, 0)

    logits_scr[...] = jnp.dot(h1b_scr[...], wr_ref[...], preferred_element_type=F32) + br_ref[...]


def _route(logits_scr, tri_scr, cnt_scr, route_ref, rt_ref, cnt_ref, valid):
    ts = logits_scr.shape[0]
    logits = logits_scr[...]
    lane = lax.broadcasted_iota(jnp.int32, (ts, LANES), 1).astype(F32)
    neg = jnp.float32(-jnp.inf)
    big = jnp.float32(LANES)

    gmask = lane < N_GROUPS
    gl = jnp.where(gmask, logits, neg)
    gmax = jnp.max(gl, axis=-1, keepdims=True)
    gidx = jnp.min(jnp.where(gl == gmax, lane, big), axis=-1, keepdims=True)
    sumexp = jnp.sum(jnp.where(gmask, jnp.exp(gl - gmax), 0.0), axis=-1, keepdims=True)
    pg = 1.0 / sumexp

    lo = N_GROUPS + EPG * gidx
    emask = (lane >= lo) & (lane < lo + EPG)
    el = jnp.where(emask, logits, neg)
    v1 = jnp.max(el, axis=-1, keepdims=True)
    i1 = jnp.min(jnp.where(el == v1, lane, big), axis=-1, keepdims=True)
    el2 = jnp.where(lane == i1, neg, el)
    v2 = jnp.max(el2, axis=-1, keepdims=True)
    i2 = jnp.min(jnp.where(el2 == v2, lane, big), axis=-1, keepdims=True)
    ex = jnp.exp(v2 - v1)
    den = 1.0 / (1.0 + ex)
    g1 = pg * den
    g2 = pg * ex * den
    e1 = i1 - N_GROUPS
    e2 = i2 - N_GROUPS

    sel1 = lane == e1
    sel2 = lane == e2
    onehot = jnp.where(sel1 | sel2, 1.0, 0.0)
    before = jnp.dot(tri_scr[...], onehot.astype(BF16), preferred_element_type=F32) + cnt_scr[...]
    rank1 = jnp.sum(jnp.where(sel1, before, 0.0), axis=-1, keepdims=True)
    rank2 = jnp.sum(jnp.where(sel2, before, 0.0), axis=-1, keepdims=True)
    cnt_scr[...] = cnt_scr[...] + valid * jnp.sum(onehot, axis=0, keepdims=True)

    slab = jnp.where(lane == 0, e1, 0.0)
    slab = jnp.where(lane == 1, e2, slab)
    slab = jnp.where(lane == 2, rank1, slab)
    slab = jnp.where(lane == 3, rank2, slab)
    slab = jnp.where(lane == 4, g1, slab)
    slab = jnp.where(lane == 5, g2, slab)
    route_ref[...] = slab
    rt_ref[0] = _exact_transpose_8(slab)
    cnt_ref[...] = cnt_scr[...]


def _seg_load(ref3, c0, g):
    return jnp.concatenate(
        [ref3[c0 + c, pl.ds(g, SUBLANES, stride=SEG), :] for c in range(NCHUNK)], axis=1)


def _seg_store(ref3, g, val):
    for c in range(NCHUNK):
        ref3[c, pl.ds(g, SUBLANES, stride=SEG), :] = val[:, c * LANES:(c + 1) * LANES]


def _rglru_kernel(x_ref, meta_ref, win_ref, cw_ref, cb_ref, wa_ref, wi_ref, ba_ref, bi_ref, lam_ref,
                  wout_ref, lng_ref, lnb_ref, wr_ref, br_ref,
                  h1_ref, route_ref, rt_ref, cnt_ref,
                  h_scr, u_scr, xc_scr, xcb_scr, g_scr, y_scr, halo_scr, carry_scr, tri_scr, cnt_scr,
                  logits_scr):
    s = pl.program_id(0)
    j = s % (SEQ_TOT // TS)

    @pl.when(s == 0)
    def _():
        _route_init(tri_scr, cnt_scr, logits_scr)

    @pl.when(j == 0)
    def _():
        halo_scr[...] = jnp.zeros_like(halo_scr)
        carry_scr[...] = jnp.zeros_like(carry_scr)
        h_scr[0:N_META_TOK, :] = meta_ref[...]
        h_scr[N_META_TOK:TS, :] = x_ref[0, 0:TS - N_META_TOK, :]

    @pl.when(j > 0)
    def _():
        h_scr[...] = x_ref[0]

    u = jnp.dot(h_scr[...].astype(BF16), win_ref[...], preferred_element_type=F32)
    for c in range(2 * NCHUNK):
        u_scr[c] = u[:, c * LANES:(c + 1) * LANES]

    _route(logits_scr, tri_scr, cnt_scr, route_ref, rt_ref, cnt_ref, jnp.where(s > 0, 1.0, 0.0))

    sub = lax.broadcasted_iota(jnp.int32, (SUBLANES, D), 0)
    cw = [cw_ref[k:k + 1, :] for k in range(4)]
    cb = cb_ref[...]

    tails = []
    for m in range(3):
        um = _seg_load(u_scr, NCHUNK, SEG - 3 + m)
        tails.append(jnp.where(sub >= 1, pltpu.roll(um, 1, 0),
                               pltpu.roll(halo_scr[m * SUBLANES:(m + 1) * SUBLANES, :], 1, 0)))
        halo_scr[m * SUBLANES:(m + 1) * SUBLANES, :] = um

    def conv_body(i, carry):
        x1, x2, x3 = carry
        outs = []
        for hh in range(CH // SUBLANES):
            cur = _seg_load(u_scr, NCHUNK, 2 * i + hh)
            outs.append(cur * cw[3] + x1 * cw[2] + x2 * cw[1] + x3 * cw[0] + cb)
            x1, x2, x3 = cur, x1, x2
        xc = jnp.concatenate(outs, axis=0)
        r = pl.multiple_of(i * CH, CH)
        xc_scr[pl.ds(r, CH), :] = xc
        xcb_scr[pl.ds(r, CH), :] = xc.astype(BF16)
        return x1, x2, x3

    lax.fori_loop(0, TS // CH, conv_body, (tails[2], tails[1], tails[0]))

    for k in range(D // GATE_BLK):
        xk = xcb_scr[:, k * GATE_BLK:(k + 1) * GATE_BLK]
        g_scr[:, k * GATE_BLK:(k + 1) * GATE_BLK] = jnp.dot(xk, wa_ref[k], preferred_element_type=F32)
        g_scr[:, D + k * GATE_BLK:D + (k + 1) * GATE_BLK] = jnp.dot(xk, wi_ref[k], preferred_element_type=F32)

    ba = ba_ref[...]
    bi = bi_ref[...]
    nsp = -LRU_C * _softplus(-lam_ref[...])

    def scan_body(i, carry):
        hl, pc = carry
        for hh in range(WALK_UNROLL):
            rows = pl.ds(pl.multiple_of((i * WALK_UNROLL + hh) * SUBLANES, SUBLANES), SUBLANES)
            xc = xc_scr[rows, :]
            ra = _sigmoid(g_scr[rows, :D] + ba)
            gi = _sigmoid(g_scr[rows, D:] + bi)
            log_a = ra * nsp
            a = jnp.exp(log_a)
            m2 = -jnp.tanh(log_a) * (a * a + 1.0)
            hl = a * hl + jnp.sqrt(m2) * (gi * xc)
            pc = a * pc
            g_scr[rows, :D] = hl
            g_scr[rows, D:] = pc
        return hl, pc

    hl_end, pc_end = lax.fori_loop(0, SEG // WALK_UNROLL, scan_body,
                                   (jnp.zeros((SUBLANES, D), F32), jnp.ones((SUBLANES, D), F32)))

    c0 = carry_scr[...]
    av, bv = pc_end, hl_end
    for sh in (1, 2, 4):
        m = sub >= sh
        a_s = pltpu.roll(av, sh, 0)
        b_s = pltpu.roll(bv, sh, 0)
        bv = jnp.where(m, av * b_s + bv, bv)
        av = jnp.where(m, av * a_s, av)
    h_end = av * c0 + bv
    cin = jnp.where(sub >= 1, pltpu.roll(h_end, 1, 0), c0)
    carry_scr[...] = jnp.broadcast_to(h_end[SUBLANES - 1:SUBLANES, :], (SUBLANES, D))

    def fix_body(i, c):
        for hh in range(WALK_UNROLL):
            g = i * WALK_UNROLL + hh
            rows = pl.ds(pl.multiple_of(g * SUBLANES, SUBLANES), SUBLANES)
            hs = g_scr[rows, :D] + g_scr[rows, D:] * cin
            _seg_store(y_scr, g, hs * jax.nn.gelu(_seg_load(u_scr, 0, g)))
        return c

    lax.fori_loop(0, SEG // WALK_UNROLL, fix_body, 0)

    y = jnp.concatenate([y_scr[c] for c in range(NCHUNK)], axis=1).astype(BF16)
    xc_scr[...] = jnp.dot(y, wout_ref[...], preferred_element_type=F32)
    _ln_logits(h_scr, xc_scr, lng_ref, lnb_ref, wr_ref, br_ref, h1_ref, xcb_scr, logits_scr)


def _shortconv_kernel(h_ref, win_ref, cw_ref, wout_ref, lng_ref, lnb_ref, wr_ref, br_ref,
                      h1_ref, route_ref, rt_ref, cnt_ref,
                      u_scr, m_scr, y_scr, halo_scr, tri_scr, cnt_scr, logits_scr):
    s = pl.program_id(0)
    j = s % (SEQ_TOT // TS)
    ts = m_scr.shape[0]

    @pl.when(s == 0)
    def _():
        _route_init(tri_scr, cnt_scr, logits_scr)

    @pl.when(j == 0)
    def _():
        halo_scr[...] = jnp.zeros_like(halo_scr)

    u_scr[...] = jnp.dot(h_ref[...].astype(BF16), win_ref[...], preferred_element_type=F32)

    _route(logits_scr, tri_scr, cnt_scr, route_ref, rt_ref, cnt_ref, jnp.where(s > 0, 1.0, 0.0))

    sub = lax.broadcasted_iota(jnp.int32, (SUBLANES, D), 0)
    cw = [cw_ref[k:k + 1, :] for k in range(3)]

    def conv_body(i, prv):
        r = pl.multiple_of(i * CH, CH)
        ys = []
        for hh in range(CH // SUBLANES):
            r0 = r + SUBLANES * hh
            cur = u_scr[pl.ds(r0, SUBLANES), D:2 * D] * u_scr[pl.ds(r0, SUBLANES), 2 * D:]
            acc = cur * cw[2]
            for sh in (1, 2):
                acc = acc + _shifted(cur, prv, sh, sub) * cw[2 - sh]
            ys.append(u_scr[pl.ds(r0, SUBLANES), :D] * acc)
            prv = cur
        y_scr[pl.ds(r, CH), :] = jnp.concatenate(ys, axis=0).astype(BF16)
        return prv

    halo_scr[...] = lax.fori_loop(0, ts // CH, conv_body, halo_scr[...])

    m_scr[...] = jnp.dot(y_scr[...], wout_ref[...], preferred_element_type=F32)
    _ln_logits(h_ref, m_scr, lng_ref, lnb_ref, wr_ref, br_ref, h1_ref, y_scr, logits_scr)


def _const_spec(shape):
    n = len(shape)
    return pl.BlockSpec(shape, lambda s: (0,) * n)


def _mixer_out(n_tiles):
    out_shape = (jax.ShapeDtypeStruct(((n_tiles + 1) * TS * SUBLANES, LANES), F32),
                 jax.ShapeDtypeStruct((n_tiles * TS, LANES), F32),
                 jax.ShapeDtypeStruct((n_tiles, SUBLANES, TS), F32),
                 jax.ShapeDtypeStruct((1, LANES), F32))
    prev = lambda s: jnp.maximum(s - 1, 0)
    out_specs = (pl.BlockSpec((TS * SUBLANES, LANES), lambda s: (s, 0)),
                 pl.BlockSpec((TS, LANES), lambda s: (prev(s), 0)),
                 pl.BlockSpec((1, SUBLANES, TS), lambda s: (prev(s), 0, 0)),
                 pl.BlockSpec((1, LANES), lambda s: (0, 0)))
    return out_shape, out_specs


def _rglru_layer(x, meta, win, cw, cb, wa, wi, ba, bi, lam, wout, lng, lnb, wr, br):
    batch = x.shape[0]
    nj = SEQ_TOT // TS
    n_tiles = batch * nj
    out_shape, out_specs = _mixer_out(n_tiles)
    consts = (meta, win, cw, cb, wa, wi, ba, bi, lam, wout, lng, lnb, wr, br)

    def x_index(s):
        t = jnp.minimum(s, n_tiles - 1)
        return (t // nj, pl.multiple_of(jnp.maximum((t % nj) * TS - N_META_TOK, 0), SUBLANES), 0)

    x_spec = pl.BlockSpec((pl.Element(1), pl.Element(TS), pl.Element(D)), x_index)
    return pl.pallas_call(
        _rglru_kernel,
        grid=(n_tiles + 1,),
        in_specs=[x_spec] + [_const_spec(c.shape) for c in consts],
        out_specs=out_specs,
        out_shape=out_shape,
        scratch_shapes=[
            pltpu.VMEM((TS, D), F32),
            pltpu.VMEM((2 * NCHUNK, TS, LANES), F32),
            pltpu.VMEM((TS, D), F32),
            pltpu.VMEM((TS, D), BF16),
            pltpu.VMEM((TS, 2 * D), F32),
            pltpu.VMEM((NCHUNK, TS, LANES), F32),
            pltpu.VMEM((3 * SUBLANES, D), F32),
            pltpu.VMEM((SUBLANES, D), F32),
            pltpu.VMEM((TS, TS), BF16),
            pltpu.VMEM((1, LANES), F32),
            pltpu.VMEM((TS, LANES), F32),
        ],
        compiler_params=pltpu.CompilerParams(
            dimension_semantics=("arbitrary",), vmem_limit_bytes=VMEM_LIMIT),
        name="rglru_layer",
    )(x, *consts)


def _shortconv_layer(h, win, cw, wout, lng, lnb, wr, br):
    n_tiles = h.shape[0] // TS
    out_shape, out_specs = _mixer_out(n_tiles)
    consts = (win, cw, wout, lng, lnb, wr, br)
    return pl.pallas_call(
        _shortconv_kernel,
        grid=(n_tiles + 1,),
        in_specs=[pl.BlockSpec((TS, D), lambda s: (jnp.minimum(s, n_tiles - 1), 0))]
        + [_const_spec(c.shape) for c in consts],
        out_specs=out_specs,
        out_shape=out_shape,
        scratch_shapes=[
            pltpu.VMEM((TS, 3 * D), F32),
            pltpu.VMEM((TS, D), F32),
            pltpu.VMEM((TS, D), BF16),
            pltpu.VMEM((SUBLANES, D), F32),
            pltpu.VMEM((TS, TS), BF16),
            pltpu.VMEM((1, LANES), F32),
            pltpu.VMEM((TS, LANES), F32),
        ],
        compiler_params=pltpu.CompilerParams(
            dimension_semantics=("arbitrary",), vmem_limit_bytes=VMEM_LIMIT),
        name="shortconv_layer",
    )(h, *consts)


def _tile_rows(t):
    return pl.ds(pl.multiple_of(t * SUBLANES, SUBLANES), SUBLANES)


def _dispatch_kernel(pend_ref, padded_ref, nused_ref, dest_ref, h_ref, xs_ref, zero_scr, sem_z, sem):
    i = pl.program_id(0)
    n_blocks = xs_ref.shape[0] // (BM * SUBLANES)

    def zero_block(blk):
        start = pl.multiple_of(blk * (BM * SUBLANES), BM * SUBLANES)
        return pltpu.make_async_copy(zero_scr, xs_ref.at[pl.ds(start, BM * SUBLANES), :], sem_z)

    @pl.when(i == 0)
    def _():
        zero_scr[...] = jnp.zeros_like(zero_scr)
        for e in range(N_EXPERTS):
            @pl.when(padded_ref[e] > 0)
            def _():
                zero_block(pend_ref[e] // BM - 1).start()

        def fill_tail(blk, c):
            zero_block(blk).start()
            return c

        lax.fori_loop(nused_ref[0], n_blocks, fill_tail, 0)
        for e in range(N_EXPERTS):
            @pl.when(padded_ref[e] > 0)
            def _():
                zero_block(0).wait()

        def drain_tail(blk, c):
            zero_block(0).wait()
            return c

        lax.fori_loop(nused_ref[0], n_blocks, drain_tail, 0)

    def row_copy(t, d):
        return pltpu.make_async_copy(h_ref.at[_tile_rows(t), :], xs_ref.at[_tile_rows(d), :], sem)

    for jj in range(TTM // TS):
        def issue(i8, c, jj=jj):
            for u in range(ISSUE_UNROLL):
                tl = i8 * ISSUE_UNROLL + u
                for k in range(2):
                    row_copy(jj * TS + tl, dest_ref[0, 0, (2 * jj + k) * TS + tl]).start(priority=k)
            return c

        lax.fori_loop(0, TS // ISSUE_UNROLL, issue, 0)

    def drain(i8, c):
        for _ in range(2 * ISSUE_UNROLL):
            row_copy(0, 0).wait()
        return c

    lax.fori_loop(0, TTM // ISSUE_UNROLL, drain, 0)


def _dispatch(h1t, dest_m, pend, padded, n_used, n_rows):
    nt = dest_m.shape[0]
    return pl.pallas_call(
        _dispatch_kernel,
        grid_spec=pltpu.PrefetchScalarGridSpec(
            num_scalar_prefetch=3,
            grid=(nt,),
            in_specs=[pl.BlockSpec((1, 1, 2 * TTM), lambda i, *_: (i, 0, 0), memory_space=pltpu.SMEM),
                      pl.BlockSpec((TTM * SUBLANES, LANES), lambda i, *_: (i, 0))],
            out_specs=pl.BlockSpec(memory_space=pl.ANY),
            scratch_shapes=[pltpu.VMEM((BM * SUBLANES, LANES), F32),
                            pltpu.SemaphoreType.DMA(()),
                            pltpu.SemaphoreType.DMA(())],
        ),
        out_shape=jax.ShapeDtypeStruct((n_rows * SUBLANES, LANES), F32),
        compiler_params=pltpu.CompilerParams(dimension_semantics=("arbitrary",)),
        name="moe_dispatch",
    )(pend, padded, n_used, dest_m, h1t)


def _expert_kernel(be_ref, nu_ref, nxt_ref, slot_ref, xs_ref, wg_hbm, wu_hbm, wd_hbm, ys_ref,
                   wg_b, wu_b, wd_b, wg_f, wu_f, wd_f, sems, *, layer):
    i = pl.program_id(0)

    def fetch(e, sl):
        return (pltpu.make_async_copy(wg_hbm.at[layer, e], wg_f.at[sl], sems.at[sl]),
                pltpu.make_async_copy(wu_hbm.at[layer, e], wu_f.at[sl], sems.at[sl]),
                pltpu.make_async_copy(wd_hbm.at[layer, e], wd_f.at[sl], sems.at[sl]))

    @pl.when(i < nu_ref[0])
    def _():
        e = be_ref[i]
        sl = slot_ref[i]

        @pl.when(i == 0)
        def _():
            for cp in fetch(e, sl):
                cp.start()

        @pl.when((i == 0) | (e != be_ref[jnp.maximum(i - 1, 0)]))
        def _():
            @pl.when(nxt_ref[i] != e)
            def _():
                for cp in fetch(nxt_ref[i], 1 - sl):
                    cp.start()

            for cp in fetch(e, sl):
                cp.wait()
            wg_b[...] = wg_f[sl].astype(BF16)
            wu_b[...] = wu_f[sl].astype(BF16)
            wd_b[...] = wd_f[sl].astype(BF16)

        xb = _load_token_tiles(xs_ref, 0, BM).astype(BF16)
        g = jnp.dot(xb, wg_b[...], preferred_element_type=F32)
        u = jnp.dot(xb, wu_b[...], preferred_element_type=F32)
        hmid = (jax.nn.silu(g) * u).astype(BF16)
        _store_token_tiles(ys_ref, 0, jnp.dot(hmid, wd_b[...], preferred_element_type=F32))

    @pl.when(i >= nu_ref[0])
    def _():
        ys_ref[...] = jnp.zeros_like(ys_ref)


def _experts(xs, block_expert, n_used, wg, wu, wd, layer):
    blk_rows = BM * SUBLANES
    nb = xs.shape[0] // blk_rows
    be = block_expert
    later = jnp.where(be[None, :] > be[:, None], be[None, :], N_EXPERTS)
    nxt = jnp.min(later, axis=1)
    nxt = jnp.where(nxt == N_EXPERTS, be, nxt).astype(jnp.int32)
    change = jnp.concatenate([jnp.zeros((1,), jnp.int32), (be[1:] != be[:-1]).astype(jnp.int32)])
    slot = (jnp.cumsum(change) % 2).astype(jnp.int32)
    return pl.pallas_call(
        functools.partial(_expert_kernel, layer=layer),
        grid_spec=pltpu.PrefetchScalarGridSpec(
            num_scalar_prefetch=4,
            grid=(nb,),
            in_specs=[pl.BlockSpec((blk_rows, LANES), lambda i, be, nu, *_: (jnp.minimum(i, nu[0] - 1), 0)),
                      pl.BlockSpec(memory_space=pl.ANY),
                      pl.BlockSpec(memory_space=pl.ANY),
                      pl.BlockSpec(memory_space=pl.ANY)],
            out_specs=pl.BlockSpec((blk_rows, LANES), lambda i, *_: (i, 0)),
            scratch_shapes=[pltpu.VMEM((D, D_EXPERT), BF16),
                            pltpu.VMEM((D, D_EXPERT), BF16),
                            pltpu.VMEM((D_EXPERT, D), BF16),
                            pltpu.VMEM((2, D, D_EXPERT), F32),
                            pltpu.VMEM((2, D, D_EXPERT), F32),
                            pltpu.VMEM((2, D_EXPERT, D), F32),
                            pltpu.SemaphoreType.DMA((2,))],
        ),
        out_shape=jax.ShapeDtypeStruct(xs.shape, F32),
        compiler_params=pltpu.CompilerParams(
            dimension_semantics=("arbitrary",), vmem_limit_bytes=VMEM_LIMIT),
        name="moe_experts",
    )(block_expert, n_used, nxt, slot, xs, wg, wu, wd)


def _combine_kernel(dest_ref, dest_next_ref, h_ref, route_ref, lng_ref, lnb_ref, ys_ref, out_ref,
                    y0_scr, y1_scr, sems, *, half, lc, stride_half, stride_k):
    i = pl.program_id(0)
    n_steps = pl.num_programs(0)
    slot = i % 2
    out2d = out_ref if len(out_ref.shape) == 2 else out_ref.at[0]

    def row_copy(d, buf, sl, t):
        return pltpu.make_async_copy(ys_ref.at[_tile_rows(d), :], buf.at[sl, _tile_rows(t), :], sems.at[sl])

    n_it = half // ISSUE_UNROLL
    per_half = half // lc

    def issue_part(dref, sl, q):
        hf, p = divmod(q, per_half)

        def issue(i8, c):
            for u in range(ISSUE_UNROLL):
                tl = i8 * ISSUE_UNROLL + u
                base = hf * stride_half + tl
                row_copy(dref[0, 0, base], y0_scr, sl, hf * half + tl).start(priority=0)
                row_copy(dref[0, 0, base + stride_k], y1_scr, sl, hf * half + tl).start(priority=1)
            return c

        lax.fori_loop(p * n_it // per_half, (p + 1) * n_it // per_half, issue, 0)

    @pl.when(i == 0)
    def _():
        for q in range(2 * per_half):
            issue_part(dest_ref, slot, q)

    def drain(i8, c):
        for _ in range(2 * ISSUE_UNROLL):
            row_copy(0, y0_scr, slot, 0).wait()
        return c

    lax.fori_loop(0, 2 * n_it, drain, 0)

    g = lng_ref[...]
    bta = lnb_ref[...]
    y0_cur = y0_scr.at[slot]
    y1_cur = y1_scr.at[slot]
    for q, r in enumerate(range(0, 2 * half, lc)):
        @pl.when(i + 1 < n_steps)
        def _(q=q):
            issue_part(dest_next_ref, 1 - slot, q)

        g1 = route_ref[r:r + lc, 4:5]
        g2 = route_ref[r:r + lc, 5:6]
        y = g1 * _load_token_tiles(y0_cur, r, lc) + g2 * _load_token_tiles(y1_cur, r, lc)
        z = ALPHA * _load_token_tiles(h_ref, r, lc) + y
        out2d[r:r + lc, :] = _layer_norm_rows(z, g, bta)


def _combine_scratch(tt):
    return [pltpu.VMEM((2, tt * SUBLANES, LANES), F32),
            pltpu.VMEM((2, tt * SUBLANES, LANES), F32),
            pltpu.SemaphoreType.DMA((2,))]


def _combine(h1t, route, dest_m, ys, lng, lnb):
    nt = dest_m.shape[0]
    body = functools.partial(_combine_kernel, half=TS, lc=LCC, stride_half=2 * TS, stride_k=TS)
    return pl.pallas_call(
        body,
        grid=(nt,),
        in_specs=[pl.BlockSpec((1, 1, 2 * TTM), lambda i: (i, 0, 0), memory_space=pltpu.SMEM),
                  pl.BlockSpec((1, 1, 2 * TTM), lambda i: (jnp.minimum(i + 1, nt - 1), 0, 0),
                               memory_space=pltpu.SMEM),
                  pl.BlockSpec((TTM * SUBLANES, LANES), lambda i: (i, 0)),
                  pl.BlockSpec((TTM, LANES), lambda i: (i, 0)),
                  pl.BlockSpec((1, D), lambda i: (0, 0)),
                  pl.BlockSpec((1, D), lambda i: (0, 0)),
                  pl.BlockSpec(memory_space=pl.ANY)],
        out_specs=pl.BlockSpec((TTM, D), lambda i: (i, 0)),
        out_shape=jax.ShapeDtypeStruct((nt * TTM, D), F32),
        scratch_shapes=_combine_scratch(TTM),
        compiler_params=pltpu.CompilerParams(
            dimension_semantics=("arbitrary",), vmem_limit_bytes=VMEM_LIMIT),
        name="moe_combine",
    )(dest_m, dest_m, h1t, route, lng, lnb, ys)


def _combine_final(h1t, route, dest_f, ys, lng, lnb, batch):
    per_seq = SEQ // TF

    def tok0(i):
        return pl.multiple_of((i // per_seq) * SEQ_TOT + N_META_TOK + (i % per_seq) * TF, SUBLANES)

    body = functools.partial(_combine_kernel, half=TF // 2, lc=LCF, stride_half=TF // 2, stride_k=TF)
    return pl.pallas_call(
        body,
        grid=(batch * per_seq,),
        in_specs=[pl.BlockSpec((1, 1, 2 * TF), lambda i: (i, 0, 0), memory_space=pltpu.SMEM),
                  pl.BlockSpec((1, 1, 2 * TF), lambda i: (jnp.minimum(i + 1, batch * per_seq - 1), 0, 0),
                               memory_space=pltpu.SMEM),
                  pl.BlockSpec((pl.Element(TF * SUBLANES), pl.Element(LANES)),
                               lambda i: (pl.multiple_of(tok0(i) * SUBLANES, SUBLANES * SUBLANES), 0)),
                  pl.BlockSpec((pl.Element(TF), pl.Element(LANES)), lambda i: (tok0(i), 0)),
                  pl.BlockSpec((1, D), lambda i: (0, 0)),
                  pl.BlockSpec((1, D), lambda i: (0, 0)),
                  pl.BlockSpec(memory_space=pl.ANY)],
        out_specs=pl.BlockSpec((1, TF, D), lambda i: (i // per_seq, i % per_seq, 0)),
        out_shape=jax.ShapeDtypeStruct((batch, SEQ, D), F32),
        scratch_shapes=_combine_scratch(TF),
        compiler_params=pltpu.CompilerParams(
            dimension_semantics=("arbitrary",), vmem_limit_bytes=VMEM_LIMIT),
        name="moe_combine_final",
    )(dest_f, dest_f, h1t, route, lng, lnb, ys)


def _moe(h1t, route, rt, cnt, wg, wu, wd, layer, lng, lnb, batch, final):
    t_tot = route.shape[0]
    nj = t_tot // (batch * TS)
    n_blocks = (2 * t_tot + N_EXPERTS * (BM - 1) + BM - 1) // BM
    rt = rt.reshape(batch, nj, SUBLANES, TS)
    eid = rt[:, :, 0:2, :].astype(jnp.int32)
    rank = rt[:, :, 2:4, :].astype(jnp.int32)
    counts = cnt[0, :N_EXPERTS].astype(jnp.int32)
    padded = (counts + BM - 1) // BM * BM
    pend = jnp.cumsum(padded)
    dest = rank
    for e in range(1, N_EXPERTS):
        dest = dest + jnp.where(eid >= e, padded[e - 1], 0)
    dest_m = dest.reshape(batch * nj * TS // TTM, 1, 2 * TTM)
    n_used = pend[-1:] // BM
    blk = jnp.minimum(jnp.arange(n_blocks, dtype=jnp.int32), n_used[0] - 1) * BM
    block_expert = jnp.minimum(jnp.sum((pend[None, :] <= blk[:, None]).astype(jnp.int32), axis=1),
                               N_EXPERTS - 1)

    xs = _dispatch(h1t, dest_m, pend, padded, n_used, n_blocks * BM)
    ys = _experts(xs, block_expert, n_used, wg, wu, wd, layer)
    if not final:
        return _combine(h1t, route, dest_m, ys, lng, lnb)
    dest_f = dest.transpose(0, 2, 1, 3).reshape(batch, 2, SEQ_TOT)[:, :, N_META_TOK:]
    dest_f = dest_f.reshape(batch, 2, SEQ // TF, TF).transpose(0, 2, 1, 3).reshape(batch * (SEQ // TF), 1, 2 * TF)
    return _combine_final(h1t, route, dest_f, ys, lng, lnb, batch)


def _block_diag(w):
    per = GATE_BLK // HEAD_DIM
    w4 = w.reshape(D // GATE_BLK, per, HEAD_DIM, HEAD_DIM)
    out = jnp.zeros((D // GATE_BLK, GATE_BLK, GATE_BLK), w.dtype)
    for p in range(per):
        out = out.at[:, p * HEAD_DIM:(p + 1) * HEAD_DIM, p * HEAD_DIM:(p + 1) * HEAD_DIM].set(w4[:, p])
    return out


def _router_params(w_group, b_group, w_expert, b_expert):
    pad = LANES - N_GROUPS - N_EXPERTS
    wr = jnp.concatenate([w_group, w_expert, jnp.zeros((D, pad), F32)], axis=1).astype(BF16)
    br = jnp.concatenate([b_group, b_expert.reshape(-1), jnp.zeros((pad,), F32)])[None, :]
    return wr, br


def kernel(x, meta_tokens, lru_w_in, lru_conv_w, lru_conv_b, lru_w_a, lru_b_a, lru_w_i, lru_b_i, lru_lambda, lru_w_out, sc_w_in, sc_conv_w, sc_w_out, moe_w_group, moe_b_group, moe_w_expert, moe_b_expert, moe_w_gate, moe_w_up, moe_w_down, ln_g, ln_b):
    batch = x.shape[0]
    row = lambda v: v.reshape(1, -1)

    wr, br = _router_params(moe_w_group[0], moe_b_group[0], moe_w_expert[0], moe_b_expert[0])
    h1, route, rt, cnt = _rglru_layer(
        x, meta_tokens.astype(x.dtype), lru_w_in[0].astype(BF16), lru_conv_w[0], row(lru_conv_b[0]),
        _block_diag(lru_w_a[0]).astype(BF16), _block_diag(lru_w_i[0]).astype(BF16),
        row(lru_b_a[0]), row(lru_b_i[0]), row(lru_lambda[0]), lru_w_out[0].astype(BF16),
        row(ln_g[0, 0]), row(ln_b[0, 0]), wr, br)
    h2 = _moe(h1, route, rt, cnt, moe_w_gate, moe_w_up, moe_w_down, 0,
              row(ln_g[0, 1]), row(ln_b[0, 1]), batch, final=False)

    wr, br = _router_params(moe_w_group[1], moe_b_group[1], moe_w_expert[1], moe_b_expert[1])
    h3, route, rt, cnt = _shortconv_layer(
        h2, sc_w_in[0].astype(BF16), sc_conv_w[0], sc_w_out[0].astype(BF16),
        row(ln_g[1, 0]), row(ln_b[1, 0]), wr, br)
    return _moe(h3, route, rt, cnt, moe_w_gate, moe_w_up, moe_w_down, 1,
                row(ln_g[1, 1]), row(ln_b[1, 1]), batch, final=True)
```
